```python
import math
import jax, jax.numpy as jnp
from jax import lax
import numpy as np

D_MODEL = 1024
BATCH = 8
SEQ = 4096
DEPTH = 2

HEAD_DIM = 64
A_Q_HEADS = 6
A_KV_HEADS = 2
A_WINDOW = 128
A_BLOCK = 128
S5_GROUP = 16
S5_CHANNELS = 256
S5_GROUPS = S5_CHANNELS // S5_GROUP
S5_STATE = 64
S5_MIN_NEG = -1e-4
C_Q_HEADS = 6
C_KV_HEADS = 2
N_BRANCH = 3
CMP_BLOCK = 32
CMP_STRIDE = 16
CMP_HIDDEN = 256
SEL_BLOCK = 64
SEL_TOPK = 16
SEL_QCHUNK = 64
C_WINDOW = 512
C_BLOCK = 128
FORCE_BONUS = 1e4
A_WIDTH = A_Q_HEADS * HEAD_DIM
A_KV_WIDTH = A_KV_HEADS * HEAD_DIM
C_WIDTH = C_Q_HEADS * HEAD_DIM
C_KV_WIDTH = C_KV_HEADS * HEAD_DIM
MIX_WIDTH = A_WIDTH + S5_CHANNELS + C_WIDTH
IN_SPLITS = (A_WIDTH, A_KV_WIDTH, A_KV_WIDTH, S5_CHANNELS, C_WIDTH,
             N_BRANCH * C_KV_WIDTH, N_BRANCH * C_KV_WIDTH, C_Q_HEADS * N_BRANCH)
IN_WIDTH = sum(IN_SPLITS)
D_FF = 4 * D_MODEL
NEG_INF = -1e30

kernel_name = "hymba_swa_s5_nsa_hybrid"


def rms_norm(x, gain, eps=1e-6):
    xf = x.astype(jnp.float32)
    y = xf * lax.rsqrt(jnp.mean(xf * xf, axis=-1, keepdims=True) + eps)
    return (y * gain.astype(jnp.float32)).astype(x.dtype)


def banded_gqa_attention(q, k, v, window, block, sinks=None):
    B, S, Hkv, G, hd = q.shape
    nq = S // block
    n_prev = -(-(window - 1) // block)
    pad = n_prev * block
    kb = jnp.pad(k, ((0, 0), (pad, 0), (0, 0), (0, 0))).reshape(B, nq + n_prev, block, Hkv, hd)
    vb = jnp.pad(v, ((0, 0), (pad, 0), (0, 0), (0, 0))).reshape(B, nq + n_prev, block, Hkv, hd)
    k_band = jnp.concatenate([kb[:, j:j + nq] for j in range(n_prev + 1)], axis=2)
    v_band = jnp.concatenate([vb[:, j:j + nq] for j in range(n_prev + 1)], axis=2)
    qb = q.reshape(B, nq, block, Hkv, G, hd)
    scores = jnp.einsum('bnqhgd,bnkhd->bhgnqk', qb, k_band).astype(jnp.float32) * (hd ** -0.5)
    qpos = jnp.arange(nq)[:, None] * block + jnp.arange(block)[None, :]
    kpos = (jnp.arange(nq)[:, None] - n_prev) * block + jnp.arange((n_prev + 1) * block)[None, :]
    diff = qpos[:, :, None] - kpos[:, None, :]
    mask = (diff >= 0) & (diff < window) & (kpos[:, None, :] >= 0)
    scores = jnp.where(mask, scores, NEG_INF)
    if sinks is None:
        p = jax.nn.softmax(scores, axis=-1)
    else:
        sink = sinks.astype(jnp.float32).reshape(Hkv, G)[None, :, :, None, None, None]
        m = jnp.maximum(jnp.max(scores, axis=-1, keepdims=True), sink)
        e = jnp.exp(scores - m)
        p = e / (jnp.sum(e, axis=-1, keepdims=True) + jnp.exp(sink - m))
    out = jnp.einsum('bhgnqk,bnkhd->bnqhgd', p.astype(v.dtype), v_band)
    return out.reshape(B, S, Hkv, G, hd)


def swa_sink_mixer(q, k, v, q_gain, k_gain, sinks):
    B, S, _ = q.shape
    G = A_Q_HEADS // A_KV_HEADS
    qh = rms_norm(q.reshape(B, S, A_KV_HEADS, G, HEAD_DIM), q_gain)
    kh = rms_norm(k.reshape(B, S, A_KV_HEADS, HEAD_DIM), k_gain)
    vh = v.reshape(B, S, A_KV_HEADS, HEAD_DIM)
    o = banded_gqa_attention(qh, kh, vh, A_WINDOW, A_BLOCK, sinks)
    return o.reshape(B, S, A_WIDTH)


def s5_mixer(u, a_re, a_im, log_step, b_re, b_im, c_re, c_im, d_skip, w_glu, b_glu):
    f32 = jnp.float32
    B, S, _ = u.shape
    ug = u.reshape(B, S, S5_GROUPS, S5_GROUP).astype(f32)
    lam_re = jnp.minimum(a_re.astype(f32), S5_MIN_NEG)
    lam_im = a_im.astype(f32)
    step = jnp.exp(log_step.astype(f32))[:, None]
    mag = jnp.exp(lam_re * step)
    abar_re = mag * jnp.cos(lam_im * step)
    abar_im = mag * jnp.sin(lam_im * step)
    den = lam_re * lam_re + lam_im * lam_im
    nr = abar_re - 1.0
    coef_re = (nr * lam_re + abar_im * lam_im) / den
    coef_im = (abar_im * lam_re - nr * lam_im) / den
    b_re = b_re.astype(f32); b_im = b_im.astype(f32)
    bb_re = coef_re[..., None] * b_re - coef_im[..., None] * b_im
    bb_im = coef_re[..., None] * b_im + coef_im[..., None] * b_re
    bu_re = jnp.einsum('bsgp,gnp->bsgn', ug, bb_re)
    bu_im = jnp.einsum('bsgp,gnp->bsgn', ug, bb_im)
    ar = jnp.broadcast_to(abar_re, bu_re.shape)
    ai = jnp.broadcast_to(abar_im, bu_im.shape)

    def combine(left, right):
        ar1, ai1, br1, bi1 = left
        ar2, ai2, br2, bi2 = right
        return (ar2 * ar1 - ai2 * ai1, ar2 * ai1 + ai2 * ar1,
                ar2 * br1 - ai2 * bi1 + br2, ar2 * bi1 + ai2 * br1 + bi2)

    _, _, h_re, h_im = lax.associative_scan(combine, (ar, ai, bu_re, bu_im), axis=1)
    y = (jnp.einsum('bsgn,gpn->bsgp', h_re, c_re.astype(f32))
         - jnp.einsum('bsgn,gpn->bsgp', h_im, c_im.astype(f32)))
    y = y + d_skip.astype(f32).reshape(S5_GROUPS, S5_GROUP) * ug
    hg = jax.nn.gelu(y.reshape(B, S, S5_CHANNELS))
    out = hg * jax.nn.sigmoid(hg @ w_glu.astype(f32) + b_glu.astype(f32))
    return out.astype(u.dtype)


def compress_blocks(x, pe, w1, b1, w2, b2):
    B, S, H, hd = x.shape
    n_cmp = (S - CMP_BLOCK) // CMP_STRIDE + 1
    idx = np.arange(n_cmp)[:, None] * CMP_STRIDE + np.arange(CMP_BLOCK)[None, :]
    blk = x[:, idx] + pe[:, None, :]
    flat = jnp.transpose(blk, (0, 1, 3, 2, 4)).reshape(B, n_cmp, H, CMP_BLOCK * hd)
    return jax.nn.gelu(flat @ w1 + b1) @ w2 + b2


def cmp_to_sel_matrix(n_cmp, n_sel):
    cs = np.arange(n_cmp)[:, None] * CMP_STRIDE
    ss = np.arange(n_sel)[None, :] * SEL_BLOCK
    cover = np.clip(np.minimum(cs + CMP_BLOCK, ss + SEL_BLOCK) - np.maximum(cs, ss), 0, None)
    return (cover / CMP_BLOCK).astype(np.float32)


def nsa_mixer(q, k_all, v_all, gate_logits, q_gain, k_gain, cmp_pe, cmp_w1, cmp_b1, cmp_w2, cmp_b2):
    f32 = jnp.float32
    B, S, _ = q.shape
    G = C_Q_HEADS // C_KV_HEADS
    Hkv = C_KV_HEADS
    scale = HEAD_DIM ** -0.5
    qh = rms_norm(q.reshape(B, S, Hkv, G, HEAD_DIM), q_gain)
    k_all = k_all.reshape(B, S, N_BRANCH, Hkv, HEAD_DIM)
    v_all = v_all.reshape(B, S, N_BRANCH, Hkv, HEAD_DIM)
    t = jnp.arange(S)

    k_cmp = rms_norm(compress_blocks(k_all[:, :, 0], cmp_pe[0], cmp_w1[0], cmp_b1[0], cmp_w2[0], cmp_b2[0]), k_gain[0])
    v_cmp = compress_blocks(v_all[:, :, 0], cmp_pe[1], cmp_w1[1], cmp_b1[1], cmp_w2[1], cmp_b2[1])
    n_cmp = k_cmp.shape[1]
    s_cmp = jnp.einsum('bshgd,bchd->bhgsc', qh, k_cmp).astype(f32) * scale
    cmp_end = jnp.arange(n_cmp) * CMP_STRIDE + CMP_BLOCK - 1
    cmp_ok = cmp_end[None, :] <= t[:, None]
    p_cmp = jax.nn.softmax(jnp.where(cmp_ok, s_cmp, NEG_INF), axis=-1)
    p_cmp = p_cmp * jnp.any(cmp_ok, axis=-1)[:, None].astype(f32)
    o_cmp = jnp.einsum('bhgsc,bchd->bshgd', p_cmp.astype(v_cmp.dtype), v_cmp)

    n_sel = S // SEL_BLOCK
    topk = min(SEL_TOPK, n_sel)
    imp = jnp.einsum('bhgsc,cj->bhsj', p_cmp, jnp.asarray(cmp_to_sel_matrix(n_cmp, n_sel)))
    qblk = (t // SEL_BLOCK)[:, None]
    j = jnp.arange(n_sel)[None, :]
    forced = (j == 0) | (j == qblk) | (j == qblk - 1)
    imp = jnp.where(forced, imp + FORCE_BONUS, imp)
    imp = jnp.where(j <= qblk, imp, NEG_INF)
    _, sel_idx = lax.top_k(imp, topk)

    k_sel = rms_norm(k_all[:, :, 1], k_gain[1])
    kb = jnp.transpose(k_sel.reshape(B, n_sel, SEL_BLOCK, Hkv, HEAD_DIM), (0, 3, 1, 2, 4))
    vb = jnp.transpose(v_all[:, :, 1].reshape(B, n_sel, SEL_BLOCK, Hkv, HEAD_DIM), (0, 3, 1, 2, 4))
    n_ch = S // SEL_QCHUNK
    q_ch = jnp.moveaxis(qh.reshape(B, n_ch, SEL_QCHUNK, Hkv, G, HEAD_DIM), 1, 0)
    idx_ch = jnp.moveaxis(sel_idx.reshape(B, Hkv, n_ch, SEL_QCHUNK, topk), 2, 0)
    pos_ch = t.reshape(n_ch, SEL_QCHUNK)
    bi = jnp.arange(B)[:, None, None, None]
    hi = jnp.arange(Hkv)[None, :, None, None]
    offs = jnp.arange(SEL_BLOCK)

    def sel_chunk(args):
        qc, idx_c, pos_c = args
        kg = kb[bi, hi, idx_c]
        vg = vb[bi, hi, idx_c]
        s = jnp.einsum('bqhgd,bhqkld->bhgqkl', qc, kg).astype(f32) * scale
        kpos = idx_c[..., None] * SEL_BLOCK + offs
        ok = kpos <= pos_c[None, None, :, None, None]
        s = jnp.where(ok[:, :, None], s, NEG_INF)
        shp = s.shape
        p = jax.nn.softmax(s.reshape(shp[:4] + (shp[4] * shp[5],)), axis=-1).reshape(shp)
        return jnp.einsum('bhgqkl,bhqkld->bqhgd', p.astype(vg.dtype), vg)

    o_slc = lax.map(sel_chunk, (q_ch, idx_ch, pos_ch))
    o_slc = jnp.moveaxis(o_slc, 0, 1).reshape(B, S, Hkv, G, HEAD_DIM)

    o_win = banded_gqa_attention(qh, rms_norm(k_all[:, :, 2], k_gain[2]), v_all[:, :, 2], C_WINDOW, C_BLOCK)

    g = jax.nn.sigmoid(gate_logits.astype(f32)).reshape(B, S, Hkv, G, N_BRANCH)
    o = g[..., 0:1] * o_cmp + g[..., 1:2] * o_slc + g[..., 2:3] * o_win
    return o.reshape(B, S, C_WIDTH).astype(q.dtype)


def setup_inputs(seed: int = 0) -> dict:
    key = jax.random.key(seed)
    ks = jax.random.split(key, 40)
    f32 = jnp.float32

    def nrm(k, shape, s):
        return jax.random.normal(k, shape, f32) * s

    L, D, N, P, Gs = DEPTH, D_MODEL, S5_STATE, S5_GROUP, S5_GROUPS
    a_im = jnp.broadcast_to(math.pi * jnp.arange(N, dtype=f32), (L, Gs, N)) + nrm(ks[10], (L, Gs, N), 0.01)
    return {
        "x": nrm(ks[0], (BATCH, SEQ, D), 1.0),
        "c": nrm(ks[1], (BATCH, D), 1.0),
        "norm1_g": 1.0 + nrm(ks[2], (L, D), 0.02),
        "norm2_g": 1.0 + nrm(ks[3], (L, D), 0.02),
        "w_ada": nrm(ks[4], (L, D, 6 * D), D ** -0.5),
        "b_ada": nrm(ks[5], (L, 6 * D), 0.02),
        "w_in": nrm(ks[6], (L, D, IN_WIDTH), D ** -0.5),
        "a_q_gain": 1.0 + nrm(ks[7], (L, HEAD_DIM), 0.02),
        "a_k_gain": 1.0 + nrm(ks[8], (L, HEAD_DIM), 0.02),
        "a_sinks": nrm(ks[9], (L, A_Q_HEADS), 0.5),
        "s5_a_re": -0.5 + nrm(ks[11], (L, Gs, N), 0.01),
        "s5_a_im": a_im,
        "s5_log_step": jax.random.uniform(ks[12], (L, Gs), f32, math.log(1e-3), math.log(1e-1)),
        "s5_b_re": nrm(ks[13], (L, Gs, N, P), (2 * P) ** -0.5),
        "s5_b_im": nrm(ks[14], (L, Gs, N, P), (2 * P) ** -0.5),
        "s5_c_re": nrm(ks[15], (L, Gs, P, N), (2 * N) ** -0.5),
        "s5_c_im": nrm(ks[16], (L, Gs, P, N), (2 * N) ** -0.5),
        "s5_d": nrm(ks[17], (L, S5_CHANNELS), 1.0),
        "s5_w_glu": nrm(ks[18], (L, S5_CHANNELS, S5_CHANNELS), S5_CHANNELS ** -0.5),
        "s5_b_glu": nrm(ks[19], (L, S5_CHANNELS), 0.02),
        "c_q_gain": 1.0 + nrm(ks[20], (L, HEAD_DIM), 0.02),
        "c_k_gain": 1.0 + nrm(ks[21], (L, N_BRANCH, HEAD_DIM), 0.02),
        "cmp_pe": nrm(ks[22], (L, 2, CMP_BLOCK, HEAD_DIM), 0.1),
        "cmp_w1": nrm(ks[23], (L, 2, CMP_BLOCK * HEAD_DIM, CMP_HIDDEN), (CMP_BLOCK * HEAD_DIM) ** -0.5),
        "cmp_b1": nrm(ks[24], (L, 2, CMP_HIDDEN), 0.02),
        "cmp_w2": nrm(ks[25], (L, 2, CMP_HIDDEN, HEAD_DIM), CMP_HIDDEN ** -0.5),
        "cmp_b2": nrm(ks[26], (L, 2, HEAD_DIM), 0.02),
        "out_norm_g": 1.0 + nrm(ks[27], (L, MIX_WIDTH), 0.02),
        "w_out": nrm(ks[28], (L, MIX_WIDTH, D), MIX_WIDTH ** -0.5),
        "w_ff1": nrm(ks[29], (L, D, D_FF), D ** -0.5),
        "w_ff2": nrm(ks[30], (L, D_FF, D), D_FF ** -0.5),
    }


def reference(x, c, norm1_g, norm2_g, w_ada, b_ada, w_in, a_q_gain, a_k_gain, a_sinks,
              s5_a_re, s5_a_im, s5_log_step, s5_b_re, s5_b_im, s5_c_re, s5_c_im, s5_d, s5_w_glu, s5_b_glu,
              c_q_gain, c_k_gain, cmp_pe, cmp_w1, cmp_b1, cmp_w2, cmp_b2,
              out_norm_g, w_out, w_ff1, w_ff2):
    offsets = [int(o) for o in np.cumsum(IN_SPLITS)[:-1]]
    a_end = A_WIDTH
    b_end = A_WIDTH + S5_CHANNELS
    for l in range(DEPTH):
        mod = jax.nn.silu(c) @ w_ada[l] + b_ada[l]
        sh1, sc1, ga1, sh2, sc2, ga2 = [m[:, None, :] for m in jnp.split(mod, 6, axis=-1)]

        h = rms_norm(x, norm1_g[l]) * (1 + sc1) + sh1
        proj = h @ w_in[l]
        aq, ak, av, su, cq, ck, cv, cg = jnp.split(proj, offsets, axis=-1)
        o_a = swa_sink_mixer(aq, ak, av, a_q_gain[l], a_k_gain[l], a_sinks[l])
        o_b = s5_mixer(su, s5_a_re[l], s5_a_im[l], s5_log_step[l], s5_b_re[l], s5_b_im[l],
                       s5_c_re[l], s5_c_im[l], s5_d[l], s5_w_glu[l], s5_b_glu[l])
        o_c = nsa_mixer(cq, ck, cv, cg, c_q_gain[l], c_k_gain[l],
                        cmp_pe[l], cmp_w1[l], cmp_b1[l], cmp_w2[l], cmp_b2[l])
        mix = jnp.concatenate([rms_norm(o_a, out_norm_g[l, :a_end]),
                               rms_norm(o_b, out_norm_g[l, a_end:b_end]),
                               rms_norm(o_c, out_norm_g[l, b_end:])], axis=-1)
        x = x + ga1 * (mix @ w_out[l])

        h = rms_norm(x, norm2_g[l]) * (1 + sc2) + sh2
        x = x + ga2 * (jnp.square(jax.nn.relu(h @ w_ff1[l])) @ w_ff2[l])
    return x
```

```python
import functools
import math

import numpy as np
import jax
import jax.numpy as jnp
from jax import lax
from jax.experimental import pallas as pl
from jax.experimental.pallas import tpu as pltpu

F32 = jnp.float32
BF16 = jnp.bfloat16

HEAD_DIM = 64
KV_HEADS = 2
GROUP = 3
Q_HEADS = KV_HEADS * GROUP
ATT_WIDTH = Q_HEADS * HEAD_DIM
A_WINDOW = 128
C_WINDOW = 512
S5_GROUP = 16
S5_GROUPS = 16
S5_CHANNELS = S5_GROUP * S5_GROUPS
S5_STATE = 64
S5_WIDTH = S5_GROUPS * S5_STATE
S5_MIN_NEG = -1e-4
N_BRANCH = 3
CMP_BLOCK = 32
CMP_STRIDE = 16
CMP_HIDDEN = 256
SEL_BLOCK = 64
SEL_TOPK = 16
FORCE_BONUS = 1e4
NEG_INF = -1e30
UNSELECTED_BIAS = -32768.0
EPS = 1e-6
LANES = 128
V7X_VMEM_BYTES = 64 * 1024 * 1024
VMEM_LIMIT = V7X_VMEM_BYTES - 8 * 1024 * 1024

NORM_WIDTH = 2 * ATT_WIDTH + 3 * LANES
IN_COLS = NORM_WIDTH + 8 * LANES


def _params(*semantics):
    return pltpu.CompilerParams(dimension_semantics=semantics, vmem_limit_bytes=VMEM_LIMIT)


def _dot(a, b):
    return jnp.dot(a, b, preferred_element_type=F32)


def _dot_nt(a, b):
    return lax.dot_general(a, b, (((1,), (1,)), ((), ())), preferred_element_type=F32)


def _split_bf16(x):
    hi = x.astype(BF16)
    lo = (x - hi.astype(F32)).astype(BF16)
    return hi, lo


def _ada_kernel(c_ref, w_ref, b_ref, o_ref):
    c = c_ref[...]
    a = c * jax.nn.sigmoid(c)
    a_hi, a_lo = _split_bf16(a)
    w_hi, w_lo = _split_bf16(w_ref[...])
    acc = _dot(a_hi, w_hi) + _dot(a_lo, w_hi) + _dot(a_hi, w_lo)
    o_ref[...] = acc + b_ref[...]


def _ada_call(c, w_ada, b_ada):
    L, D, N = w_ada.shape
    B = c.shape[0]
    tn = 768
    return pl.pallas_call(
        _ada_kernel,
        grid=(L, N // tn),
        in_specs=[pl.BlockSpec((B, D), lambda l, j: (0, 0)),
                  pl.BlockSpec((None, D, tn), lambda l, j: (l, 0, j)),
                  pl.BlockSpec((None, 1, tn), lambda l, j: (l, 0, j))],
        out_specs=pl.BlockSpec((None, B, tn), lambda l, j: (l, 0, j)),
        out_shape=jax.ShapeDtypeStruct((L, B, N), F32),
        compiler_params=_params("arbitrary", "arbitrary"),
        name="adaln",
    )(c, w_ada, b_ada.reshape(L, 1, N))


def _inproj_kernel(x_ref, g_ref, sc_ref, sh_ref, w_ref, j_ref, qkg_ref,
                   qa_ref, qc_ref, ka_ref, ksel_ref, kwin_ref, va_ref, vsel_ref, vwin_ref,
                   kvc_ref, su_ref, gate_ref):
    x = x_ref[...]
    ms = jnp.mean(x * x, axis=-1, keepdims=True)
    h = x * lax.rsqrt(ms + EPS) * g_ref[...]
    h = h * (1.0 + sc_ref[...]) + sh_ref[...]
    proj = _dot(h.astype(BF16), w_ref[...])

    j2 = j_ref[...]
    normed = []
    for c0 in range(0, NORM_WIDTH, 2 * LANES):
        w = min(2 * LANES, NORM_WIDTH - c0)
        z = proj[:, c0:c0 + w]
        msq = _dot((z * z).astype(BF16), j2[:w, :w])
        normed.append(z * lax.rsqrt(msq + EPS) * qkg_ref[:, c0:c0 + w])
    zn = jnp.concatenate(normed, axis=-1).astype(BF16)

    def heads(dst_ref, src, col0, n):
        for i in range(n):
            dst_ref[i] = src[:, col0 + i * HEAD_DIM: col0 + (i + 1) * HEAD_DIM]

    heads(qa_ref, zn, 0, Q_HEADS)
    heads(qc_ref, zn, ATT_WIDTH, Q_HEADS)
    heads(ka_ref, zn, 2 * ATT_WIDTH, KV_HEADS)
    heads(ksel_ref, zn, 2 * ATT_WIDTH + LANES, KV_HEADS)
    heads(kwin_ref, zn, 2 * ATT_WIDTH + 2 * LANES, KV_HEADS)
    c = NORM_WIDTH
    vb = proj[:, c:c + 3 * LANES].astype(BF16)
    heads(va_ref, vb, 0, KV_HEADS)
    heads(vsel_ref, vb, LANES, KV_HEADS)
    heads(vwin_ref, vb, 2 * LANES, KV_HEADS)
    c += 3 * LANES
    heads(kvc_ref, proj, c, 2 * KV_HEADS)
    c += 2 * LANES
    su_ref[...] = proj[:, c:c + S5_CHANNELS]
    c += S5_CHANNELS
    gate_ref[...] = jax.nn.sigmoid(proj[:, c:c + LANES])


def _inproj_weights(w_in, a_q_gain, a_k_gain, c_q_gain, c_k_gain):
    D = w_in.shape[0]
    o = np.cumsum([0, ATT_WIDTH, LANES, LANES, S5_CHANNELS, ATT_WIDTH, 3 * LANES, 3 * LANES, Q_HEADS * N_BRANCH])
    aq, ak, av, su, cq, ck, cv, cg = [w_in[:, o[i]:o[i + 1]] for i in range(8)]
    ck = ck.reshape(D, N_BRANCH, LANES)
    cv = cv.reshape(D, N_BRANCH, LANES)
    gpad = jnp.zeros((D, LANES - cg.shape[1]), w_in.dtype)
    w = jnp.concatenate([aq, cq, ak, ck[:, 1], ck[:, 2],
                         av, cv[:, 1], cv[:, 2], ck[:, 0], cv[:, 0], su, cg, gpad], axis=1)
    scale = HEAD_DIM ** -0.5
    gains = jnp.concatenate([jnp.tile(a_q_gain, Q_HEADS) * scale, jnp.tile(c_q_gain, Q_HEADS) * scale,
                             jnp.tile(a_k_gain, KV_HEADS), jnp.tile(c_k_gain[1], KV_HEADS),
                             jnp.tile(c_k_gain[2], KV_HEADS)]).reshape(1, NORM_WIDTH)
    return w.astype(BF16), gains.astype(F32)


def _head_mean_matrix():
    n = 2 * LANES
    m = (np.arange(n)[:, None] // HEAD_DIM == np.arange(n)[None, :] // HEAD_DIM).astype(np.float32) / HEAD_DIM
    return jnp.asarray(m, BF16)


def _inproj_call(x, gain, sc, sh, w, qk_gain, tm):
    B, S, D = x.shape
    hs = lambda n, dt: jax.ShapeDtypeStruct((B, n, S, HEAD_DIM), dt)
    hspec = lambda n: pl.BlockSpec((None, n, tm, HEAD_DIM), lambda b, i: (b, 0, i, 0))
    vec = pl.BlockSpec((None, 1, D), lambda b, i: (b, 0, 0))
    return pl.pallas_call(
        _inproj_kernel,
        grid=(B, S // tm),
        in_specs=[pl.BlockSpec((None, tm, D), lambda b, i: (b, i, 0)),
                  pl.BlockSpec((1, D), lambda b, i: (0, 0)), vec, vec,
                  pl.BlockSpec((D, IN_COLS), lambda b, i: (0, 0)),
                  pl.BlockSpec((2 * LANES, 2 * LANES), lambda b, i: (0, 0)),
                  pl.BlockSpec((1, NORM_WIDTH), lambda b, i: (0, 0))],
        out_specs=[hspec(Q_HEADS), hspec(Q_HEADS), hspec(KV_HEADS), hspec(KV_HEADS), hspec(KV_HEADS),
                   hspec(KV_HEADS), hspec(KV_HEADS), hspec(KV_HEADS), hspec(2 * KV_HEADS),
                   pl.BlockSpec((tm, S5_CHANNELS), lambda b, i: (i, b)),
                   pl.BlockSpec((None, tm, LANES), lambda b, i: (b, i, 0))],
        out_shape=[hs(Q_HEADS, BF16), hs(Q_HEADS, BF16), hs(KV_HEADS, BF16), hs(KV_HEADS, BF16),
                   hs(KV_HEADS, BF16), hs(KV_HEADS, BF16), hs(KV_HEADS, BF16), hs(KV_HEADS, BF16),
                   hs(2 * KV_HEADS, F32),
                   jax.ShapeDtypeStruct((S, B * S5_CHANNELS), F32),
                   jax.ShapeDtypeStruct((B, S, LANES), F32)],
        compiler_params=_params("arbitrary", "arbitrary"),
        name="inproj",
    )(x, gain.reshape(1, D), sc, sh, w, _head_mean_matrix(), qk_gain)


def _band_kernel(*refs, tq, span, window, has_sink, gate_branch, has_norm):
    refs = list(refs)
    sink_ref = refs.pop(0) if has_sink else None
    q_ref, k_ref, v_ref = refs[:3]
    refs = refs[3:]
    gate_ref = refs.pop(0) if gate_branch is not None else None
    gain_ref = refs.pop(0) if has_norm else None
    o_ref = refs[0]

    q0 = pl.program_id(1) * tq
    start = pl.multiple_of(jnp.maximum(q0 + tq - span, 0), LANES)
    rows = lax.broadcasted_iota(jnp.int32, (GROUP * tq, span), 0)
    cols = lax.broadcasted_iota(jnp.int32, (GROUP * tq, span), 1)
    diff = (q0 + (rows & (tq - 1))) - (start + cols)
    mask = jnp.where(diff >= 0, diff, window) < window
    if has_sink:
        row1 = lax.broadcasted_iota(jnp.int32, (GROUP * tq, 1), 0)

    outs = []
    for h in range(KV_HEADS):
        q = q_ref[h * GROUP:(h + 1) * GROUP].reshape(GROUP * tq, HEAD_DIM)
        k = k_ref[h, pl.ds(start, span), :]
        v = v_ref[h, pl.ds(start, span), :]
        s = jnp.where(mask, _dot_nt(q, k), NEG_INF)
        m = jnp.max(s, axis=-1, keepdims=True)
        if has_sink:
            sink = jnp.zeros((GROUP * tq, 1), F32)
            for g in range(GROUP):
                sink = jnp.where(row1 >= g * tq, sink_ref[h * GROUP + g], sink)
            m = jnp.maximum(m, sink)
        e = jnp.exp(s - m)
        den = jnp.sum(e, axis=-1, keepdims=True)
        if has_sink:
            den = den + jnp.exp(sink - m)
        o = _dot(e.astype(BF16), v) / den
        for g in range(GROUP):
            og = o[g * tq:(g + 1) * tq]
            if gate_branch is not None:
                col = (h * GROUP + g) * N_BRANCH + gate_branch
                og = og * gate_ref[:, col:col + 1]
            outs.append(og)
    o_all = jnp.concatenate(outs, axis=-1)
    if has_norm:
        ms = jnp.mean(o_all * o_all, axis=-1, keepdims=True)
        o_all = o_all * lax.rsqrt(ms + EPS) * gain_ref[...]
    o_ref[...] = o_all.astype(o_ref.dtype)


def _band_call(q, k, v, *, window, sinks=None, gates=None, gate_branch=None, norm_gain=None, out_dtype=F32):
    B, _, S, _ = q.shape
    tq = LANES
    span = min(S, tq + -(-(window - 1) // LANES) * LANES)
    kern = functools.partial(_band_kernel, tq=tq, span=span, window=window, has_sink=sinks is not None,
                             gate_branch=gate_branch, has_norm=norm_gain is not None)
    in_specs, args = [], []
    if sinks is not None:
        in_specs.append(pl.BlockSpec(memory_space=pltpu.SMEM))
        args.append(sinks.astype(F32))
    in_specs += [pl.BlockSpec((None, Q_HEADS, tq, HEAD_DIM), lambda b, i: (b, 0, i, 0)),
                 pl.BlockSpec((None, KV_HEADS, S, HEAD_DIM), lambda b, i: (b, 0, 0, 0)),
                 pl.BlockSpec((None, KV_HEADS, S, HEAD_DIM), lambda b, i: (b, 0, 0, 0))]
    args += [q, k, v]
    if gates is not None:
        in_specs.append(pl.BlockSpec((None, tq, LANES), lambda b, i: (b, i, 0)))
        args.append(gates)
    if norm_gain is not None:
        in_specs.append(pl.BlockSpec((1, ATT_WIDTH), lambda b, i: (0, 0)))
        args.append(norm_gain.reshape(1, ATT_WIDTH).astype(F32))
    return pl.pallas_call(
        kern,
        grid=(B, S // tq),
        in_specs=in_specs,
        out_specs=pl.BlockSpec((None, tq, ATT_WIDTH), lambda b, i: (b, i, 0)),
        out_shape=jax.ShapeDtypeStruct((B, S, ATT_WIDTH), out_dtype),
        compiler_params=_params("arbitrary", "arbitrary"),
        name=f"band_attn_w{window}",
    )(*args)


def _s5_kernel(u_ref, bm_ref, are_ref, aim_ref, cm_ref, d_ref, wg_ref, bg_ref, gain_ref, o_ref,
               h_ref, st_ref, *, tc, nb):
    @pl.when(pl.program_id(0) == 0)
    def _():
        st_ref[...] = jnp.zeros_like(st_ref)

    u = u_ref[...]
    h_ref[...] = _dot(u.astype(BF16), bm_ref[...])
    a_re = are_ref[...]
    a_im = aim_ref[...]

    def step(t, carry):
        h_re, h_im = carry
        r = pl.ds(pl.multiple_of(t * nb, nb), nb)
        b_re = h_ref[r, 0:S5_WIDTH]
        b_im = h_ref[r, S5_WIDTH:2 * S5_WIDTH]
        n_re = a_re * h_re - a_im * h_im + b_re
        n_im = a_re * h_im + a_im * h_re + b_im
        h_ref[r, 0:S5_WIDTH] = n_re
        h_ref[r, S5_WIDTH:2 * S5_WIDTH] = n_im
        return n_re, n_im

    h_re, h_im = lax.fori_loop(0, tc, step, (st_ref[:, 0:S5_WIDTH], st_ref[:, S5_WIDTH:2 * S5_WIDTH]))
    st_ref[:, 0:S5_WIDTH] = h_re
    st_ref[:, S5_WIDTH:2 * S5_WIDTH] = h_im

    y = _dot(h_ref[...].astype(BF16), cm_ref[...]) + d_ref[...] * u
    hg = jax.nn.gelu(y)
    out = hg * jax.nn.sigmoid(_dot(hg.astype(BF16), wg_ref[...]) + bg_ref[...])
    ms = jnp.mean(out * out, axis=-1, keepdims=True)
    o_ref[...] = (out * lax.rsqrt(ms + EPS) * gain_ref[...]).astype(o_ref.dtype)


def _s5_weights(a_re, a_im, log_step, b_re, b_im, c_re, c_im):
    lam_re = jnp.minimum(a_re, S5_MIN_NEG)
    lam_im = a_im
    step = jnp.exp(log_step)[:, None]
    mag = jnp.exp(lam_re * step)
    abar_re = mag * jnp.cos(lam_im * step)
    abar_im = mag * jnp.sin(lam_im * step)
    den = lam_re * lam_re + lam_im * lam_im
    nr = abar_re - 1.0
    coef_re = (nr * lam_re + abar_im * lam_im) / den
    coef_im = (abar_im * lam_re - nr * lam_im) / den
    bb_re = coef_re[..., None] * b_re - coef_im[..., None] * b_im
    bb_im = coef_re[..., None] * b_im + coef_im[..., None] * b_re
    eye = jnp.eye(S5_GROUPS, dtype=F32)
    bd = lambda m: jnp.einsum('gnp,gh->gphn', m, eye).reshape(S5_CHANNELS, S5_WIDTH)
    bmat = jnp.concatenate([bd(bb_re), bd(bb_im)], axis=1)
    cd = lambda m: jnp.einsum('gpn,gh->gnhp', m, eye).reshape(S5_WIDTH, S5_CHANNELS)
    cmat = jnp.concatenate([cd(c_re), -cd(c_im)], axis=0)
    return bmat.astype(BF16), abar_re.reshape(1, S5_WIDTH), abar_im.reshape(1, S5_WIDTH), cmat.astype(BF16)


def _s5_call(u_tb, nb, bmat, abar_re, abar_im, cmat, d_skip, w_glu, b_glu, gain, tc):
    rows = u_tb.shape[0]
    S = rows // nb
    C = S5_CHANNELS
    const = lambda shape: pl.BlockSpec(shape, lambda i: (0,) * len(shape))
    return pl.pallas_call(
        functools.partial(_s5_kernel, tc=tc, nb=nb),
        grid=(S // tc,),
        in_specs=[pl.BlockSpec((tc * nb, C), lambda i: (i, 0)),
                  const((C, 2 * S5_WIDTH)), const((nb, S5_WIDTH)), const((nb, S5_WIDTH)),
                  const((2 * S5_WIDTH, C)), const((1, C)), const((C, C)), const((1, C)), const((1, C))],
        out_specs=pl.BlockSpec((tc * nb, C), lambda i: (i, 0)),
        out_shape=jax.ShapeDtypeStruct((rows, C), BF16),
        scratch_shapes=[pltpu.VMEM((tc * nb, 2 * S5_WIDTH), F32), pltpu.VMEM((nb, 2 * S5_WIDTH), F32)],
        compiler_params=_params("arbitrary"),
        name="s5_mixer",
    )(u_tb, bmat, jnp.broadcast_to(abar_re, (nb, S5_WIDTH)), jnp.broadcast_to(abar_im, (nb, S5_WIDTH)),
      cmat, d_skip.reshape(1, C), w_glu.astype(BF16), b_glu.reshape(1, C), gain.reshape(1, C))


def _compress_kernel(x_ref, pe_ref, w1_ref, b1_ref, w2_ref, b2_ref, kg_ref, k_ref, v_ref):
    half = CMP_STRIDE * HEAD_DIM
    n = x_ref.shape[1]
    for seg in range(2 * KV_HEADS):
        kind = seg // KV_HEADS
        x = x_ref[seg]
        z0 = _dot((x + pe_ref[kind, 0:1]).astype(BF16), w1_ref[kind, 0:half])
        z1 = _dot((x + pe_ref[kind, 1:2]).astype(BF16), w1_ref[kind, half:2 * half])
        z = z0 + pltpu.roll(z1, n - 1, 0)
        act = jax.nn.gelu(z + b1_ref[kind])
        out = _dot(act.astype(BF16), w2_ref[kind]) + b2_ref[kind]
        if kind == 0:
            ms = jnp.mean(out * out, axis=-1, keepdims=True)
            k_ref[seg] = (out * lax.rsqrt(ms + EPS) * kg_ref[...]).astype(BF16)
        else:
            v_ref[seg - KV_HEADS] = out.astype(BF16)


def _compress_call(kvc, pe, w1, b1, w2, b2, k_gain0):
    B, _, S, _ = kvc.shape
    n = S // CMP_STRIDE
    half = CMP_STRIDE * HEAD_DIM
    x = kvc.reshape(B, 2 * KV_HEADS, n, half)
    const = lambda shape: pl.BlockSpec(shape, lambda b: (0,) * len(shape))
    out = jax.ShapeDtypeStruct((B, KV_HEADS, n, HEAD_DIM), BF16)
    return pl.pallas_call(
        _compress_kernel,
        grid=(B,),
        in_specs=[pl.BlockSpec((None, 2 * KV_HEADS, n, half), lambda b: (b, 0, 0, 0)),
                  const((2, 2, half)), const((2, 2 * half, CMP_HIDDEN)), const((2, 1, CMP_HIDDEN)),
                  const((2, CMP_HIDDEN, HEAD_DIM)), const((2, 1, HEAD_DIM)), const((1, HEAD_DIM))],
        out_specs=[pl.BlockSpec((None, KV_HEADS, n, HEAD_DIM), lambda b: (b, 0, 0, 0))] * 2,
        out_shape=[out, out],
        compiler_params=_params("arbitrary"),
        name="nsa_compress",
    )(x, pe.reshape(2, 2, half), w1.astype(BF16), b1.reshape(2, 1, CMP_HIDDEN), w2.astype(BF16),
      b2.reshape(2, 1, HEAD_DIM), k_gain0.reshape(1, HEAD_DIM))


def _cmp_select_kernel(q_ref, k_ref, v_ref, cov_ref, eye_ref, gate_ref, o_ref, sel_ref, *, tq, topk):
    n_cmp = k_ref.shape[1]
    n_sel = cov_ref.shape[0]
    q0 = pl.program_id(1) * tq
    rows = lax.broadcasted_iota(jnp.int32, (GROUP * tq, n_cmp), 0)
    cidx = lax.broadcasted_iota(jnp.int32, (GROUP * tq, n_cmp), 1)
    ok = cidx * CMP_STRIDE + (CMP_BLOCK - 1) <= q0 + (rows & (tq - 1))
    t_row = q0 + (lax.broadcasted_iota(jnp.int32, (GROUP * tq, 1), 0) & (tq - 1))
    any_ok = (t_row >= CMP_BLOCK - 1).astype(F32)

    j = lax.broadcasted_iota(jnp.int32, (n_sel, tq), 0)
    qblk = (q0 + lax.broadcasted_iota(jnp.int32, (n_sel, tq), 1)) >> (SEL_BLOCK.bit_length() - 1)
    forced = (j == 0) | (j == qblk) | (j == qblk - 1)

    outs = []
    for h in range(KV_HEADS):
        q = q_ref[h * GROUP:(h + 1) * GROUP].reshape(GROUP * tq, HEAD_DIM)
        s = jnp.where(ok, _dot_nt(q, k_ref[h]), NEG_INF)
        m = jnp.max(s, axis=-1, keepdims=True)
        e = jnp.exp(s - m)
        p = e / jnp.sum(e, axis=-1, keepdims=True) * any_ok
        o = _dot(p.astype(BF16), v_ref[h])
        for g in range(GROUP):
            col = (h * GROUP + g) * N_BRANCH
            outs.append(o[g * tq:(g + 1) * tq] * gate_ref[:, col:col + 1])

        psum = p[0:tq] + p[tq:2 * tq] + p[2 * tq:3 * tq]
        p_hi, p_lo = _split_bf16(psum)
        imp = _dot_nt(cov_ref[...], p_hi) + _dot_nt(cov_ref[...], p_lo)
        imp = jnp.where(forced, imp + FORCE_BONUS, imp)
        imp = jnp.where(j <= qblk, imp, NEG_INF)
        rank = jnp.zeros((n_sel, tq), F32)
        for i in range(n_sel):
            r = imp[i:i + 1, :]
            rank = rank + jnp.where(j > i, jnp.where(r >= imp, 1.0, 0.0), jnp.where(r > imp, 1.0, 0.0))
        bias_t = jnp.where(rank < topk, 0.0, UNSELECTED_BIAS).astype(BF16)
        sel_ref[h] = _dot_nt(eye_ref[...], bias_t).astype(BF16)
    o_ref[...] = jnp.concatenate(outs, axis=-1)


def _cover_matrix_t(n_cmp, n_sel):
    cs = np.arange(n_cmp)[None, :] * CMP_STRIDE
    ss = np.arange(n_sel)[:, None] * SEL_BLOCK
    cover = np.clip(np.minimum(cs + CMP_BLOCK, ss + SEL_BLOCK) - np.maximum(cs, ss), 0, None)
    return jnp.asarray(cover / CMP_BLOCK, BF16)


def _cmp_select_call(q, k_cmp, v_cmp, gates):
    B, _, S, _ = q.shape
    tq = LANES
    n_cmp = k_cmp.shape[2]
    n_sel = S // SEL_BLOCK
    topk = min(SEL_TOPK, n_sel)
    return pl.pallas_call(
        functools.partial(_cmp_select_kernel, tq=tq, topk=topk),
        grid=(B, S // tq),
        in_specs=[pl.BlockSpec((None, Q_HEADS, tq, HEAD_DIM), lambda b, i: (b, 0, i, 0)),
                  pl.BlockSpec((None, KV_HEADS, n_cmp, HEAD_DIM), lambda b, i: (b, 0, 0, 0)),
                  pl.BlockSpec((None, KV_HEADS, n_cmp, HEAD_DIM), lambda b, i: (b, 0, 0, 0)),
                  pl.BlockSpec((n_sel, n_cmp), lambda b, i: (0, 0)),
                  pl.BlockSpec((tq, tq), lambda b, i: (0, 0)),
                  pl.BlockSpec((None, tq, LANES), lambda b, i: (b, i, 0))],
        out_specs=[pl.BlockSpec((None, tq, ATT_WIDTH), lambda b, i: (b, i, 0)),
                   pl.BlockSpec((None, KV_HEADS, tq, n_sel), lambda b, i: (b, 0, i, 0))],
        out_shape=[jax.ShapeDtypeStruct((B, S, ATT_WIDTH), F32),
                   jax.ShapeDtypeStruct((B, KV_HEADS, S, n_sel), BF16)],
        compiler_params=_params("arbitrary", "arbitrary"),
        name="nsa_cmp_select",
    )(q, k_cmp, v_cmp, _cover_matrix_t(n_cmp, n_sel), jnp.eye(tq, dtype=BF16), gates)


def _selected_kernel(q_ref, k_ref, v_ref, sel_ref, exp_ref, gate_ref, o_ref, m_ref, l_ref, acc_ref, *, tq, tk):
    q0 = pl.program_id(1) * tq
    n_kt = (q0 + tq - 1) // tk + 1
    rows = lax.broadcasted_iota(jnp.int32, (GROUP * tq, tk), 0)
    cols = lax.broadcasted_iota(jnp.int32, (GROUP * tq, tk), 1)
    qpos = q0 + (rows & (tq - 1))

    outs = []
    for h in range(KV_HEADS):
        q = q_ref[h * GROUP:(h + 1) * GROUP].reshape(GROUP * tq, HEAD_DIM)
        sel = sel_ref[h]
        m_ref[...] = jnp.full_like(m_ref, NEG_INF)
        l_ref[...] = jnp.zeros_like(l_ref)
        acc_ref[...] = jnp.zeros_like(acc_ref)

        def body(kt, carry):
            k0 = pl.multiple_of(kt * tk, tk)
            k = k_ref[h, pl.ds(k0, tk), :]
            v = v_ref[h, pl.ds(k0, tk), :]
            bias = _dot(sel, exp_ref[kt])
            s = _dot_nt(q, k) + jnp.concatenate([bias] * GROUP, axis=0)
            s = jnp.where(k0 + cols <= qpos, s, NEG_INF)
            m_prev = m_ref[...]
            m_new = jnp.maximum(m_prev, jnp.max(s, axis=-1, keepdims=True))
            alpha = jnp.exp(m_prev - m_new)
            p = jnp.exp(s - m_new)
            l_ref[...] = alpha * l_ref[...] + jnp.sum(p, axis=-1, keepdims=True)
            acc_ref[...] = alpha * acc_ref[...] + _dot(p.astype(BF16), v)
            m_ref[...] = m_new
            return carry

        lax.fori_loop(0, n_kt, body, 0)
        o = acc_ref[...] / l_ref[...]
        for g in range(GROUP):
            col = (h * GROUP + g) * N_BRANCH + 1
            outs.append(o[g * tq:(g + 1) * tq] * gate_ref[:, col:col + 1])
    o_ref[...] = jnp.concatenate(outs, axis=-1)


def _block_expand(S, tk):
    n_sel = S // SEL_BLOCK
    kpos = np.arange(S).reshape(S // tk, 1, tk)
    return jnp.asarray((kpos // SEL_BLOCK == np.arange(n_sel)[None, :, None]).astype(np.float32), BF16)


def _selected_call(q, k, v, sel, gates):
    B, _, S, _ = q.shape
    tq = LANES
    tk = min(2 * LANES, S)
    n_sel = S // SEL_BLOCK
    return pl.pallas_call(
        functools.partial(_selected_kernel, tq=tq, tk=tk),
        grid=(B, S // tq),
        in_specs=[pl.BlockSpec((None, Q_HEADS, tq, HEAD_DIM), lambda b, i: (b, 0, i, 0)),
                  pl.BlockSpec((None, KV_HEADS, S, HEAD_DIM), lambda b, i: (b, 0, 0, 0)),
                  pl.BlockSpec((None, KV_HEADS, S, HEAD_DIM), lambda b, i: (b, 0, 0, 0)),
                  pl.BlockSpec((None, KV_HEADS, tq, n_sel), lambda b, i: (b, 0, i, 0)),
                  pl.BlockSpec((S // tk, n_sel, tk), lambda b, i: (0, 0, 0)),
                  pl.BlockSpec((None, tq, LANES), lambda b, i: (b, i, 0))],
        out_specs=pl.BlockSpec((None, tq, ATT_WIDTH), lambda b, i: (b, i, 0)),
        out_shape=jax.ShapeDtypeStruct((B, S, ATT_WIDTH), F32),
        scratch_shapes=[pltpu.VMEM((GROUP * tq, 1), F32), pltpu.VMEM((GROUP * tq, 1), F32),
                        pltpu.VMEM((GROUP * tq, HEAD_DIM), F32)],
        compiler_params=_params("arbitrary", "arbitrary"),
        name="nsa_selected",
    )(q, k, v, sel, _block_expand(S, tk), gates)


def _mix_kernel(oa_ref, ob_ref, c1_ref, c2_ref, c3_ref, gc_ref, w_ref, x_ref, ga_ref, o_ref):
    oc = c1_ref[...] + c2_ref[...] + c3_ref[...]
    ms = jnp.mean(oc * oc, axis=-1, keepdims=True)
    oc = (oc * lax.rsqrt(ms + EPS) * gc_ref[...]).astype(BF16)
    a_end = ATT_WIDTH
    b_end = ATT_WIDTH + S5_CHANNELS
    y = (_dot(oa_ref[...], w_ref[0:a_end]) + _dot(ob_ref[...], w_ref[a_end:b_end])
         + _dot(oc, w_ref[b_end:]))
    o_ref[...] = x_ref[...] + ga_ref[...] * y


def _mix_call(o_a, o_b_tb, o_cmp, o_slc, o_win, gain_c, w_out, x, ga, tm):
    B, S, D = x.shape
    att = pl.BlockSpec((None, tm, ATT_WIDTH), lambda b, i: (b, i, 0))
    return pl.pallas_call(
        _mix_kernel,
        grid=(B, S // tm),
        in_specs=[att, pl.BlockSpec((tm, S5_CHANNELS), lambda b, i: (i, b)), att, att, att,
                  pl.BlockSpec((1, ATT_WIDTH), lambda b, i: (0, 0)),
                  pl.BlockSpec(w_out.shape, lambda b, i: (0, 0)),
                  pl.BlockSpec((None, tm, D), lambda b, i: (b, i, 0)),
                  pl.BlockSpec((None, 1, D), lambda b, i: (b, 0, 0))],
        out_specs=pl.BlockSpec((None, tm, D), lambda b, i: (b, i, 0)),
        out_shape=jax.ShapeDtypeStruct((B, S, D), F32),
        compiler_params=_params("arbitrary", "arbitrary"),
        name="mix_outproj",
    )(o_a, o_b_tb, o_cmp, o_slc, o_win, gain_c.reshape(1, ATT_WIDTH), w_out, x, ga)


def _ffn_kernel(x_ref, g_ref, sc_ref, sh_ref, ga_ref, w1_ref, w2_ref, o_ref, h_ref, acc_ref):
    j = pl.program_id(2)

    @pl.when(j == 0)
    def _():
        x = x_ref[...]
        ms = jnp.mean(x * x, axis=-1, keepdims=True)
        h = x * lax.rsqrt(ms + EPS) * g_ref[...]
        h_ref[...] = (h * (1.0 + sc_ref[...]) + sh_ref[...]).astype(BF16)
        acc_ref[...] = jnp.zeros_like(acc_ref)

    a = jnp.maximum(_dot(h_ref[...], w1_ref[...]), 0.0)
    acc_ref[...] += _dot((a * a).astype(BF16), w2_ref[...])

    @pl.when(j == pl.num_programs(2) - 1)
    def _():
        o_ref[...] = x_ref[...] + ga_ref[...] * acc_ref[...]


def _ffn_call(x, gain, sc, sh, ga, w1, w2, tm, tf):
    B, S, D = x.shape
    F = w1.shape[1]
    vec = pl.BlockSpec((None, 1, D), lambda b, i, j: (b, 0, 0))
    return pl.pallas_call(
        _ffn_kernel,
        grid=(B, S // tm, F // tf),
        in_specs=[pl.BlockSpec((None, tm, D), lambda b, i, j: (b, i, 0)),
                  pl.BlockSpec((1, D), lambda b, i, j: (0, 0)), vec, vec, vec,
                  pl.BlockSpec((D, tf), lambda b, i, j: (0, j)),
                  pl.BlockSpec((tf, D), lambda b, i, j: (j, 0))],
        out_specs=pl.BlockSpec((None, tm, D), lambda b, i, j: (b, i, 0)),
        out_shape=jax.ShapeDtypeStruct((B, S, D), F32),
        scratch_shapes=[pltpu.VMEM((tm, D), BF16), pltpu.VMEM((tm, D), F32)],
        compiler_params=_params("arbitrary", "arbitrary", "arbitrary"),
        name="ffn",
    )(x, gain.reshape(1, D), sc, sh, ga, w1, w2)


def _layer(x, mod, p, tiles):
    B, S, D = x.shape
    sh1, sc1, ga1, sh2, sc2, ga2 = [m.reshape(B, 1, D) for m in jnp.split(mod, 6, axis=-1)]

    w_in, qk_gain = _inproj_weights(p["w_in"], p["a_q_gain"], p["a_k_gain"], p["c_q_gain"], p["c_k_gain"])
    (qa, qc, ka, ksel, kwin, va, vsel, vwin, kvc, su, gates) = _inproj_call(
        x, p["norm1_g"], sc1, sh1, w_in, qk_gain, tiles["tm_in"])

    a_end = ATT_WIDTH
    b_end = ATT_WIDTH + S5_CHANNELS
    o_a = _band_call(qa, ka, va, window=A_WINDOW, sinks=p["a_sinks"], norm_gain=p["out_norm_g"][:a_end],
                     out_dtype=BF16)

    bmat, abar_re, abar_im, cmat = _s5_weights(p["s5_a_re"], p["s5_a_im"], p["s5_log_step"],
                                               p["s5_b_re"], p["s5_b_im"], p["s5_c_re"], p["s5_c_im"])
    o_b = _s5_call(su.reshape(S * B, S5_CHANNELS), B, bmat, abar_re, abar_im, cmat, p["s5_d"],
                   p["s5_w_glu"], p["s5_b_glu"], p["out_norm_g"][a_end:b_end], tiles["tc_s5"])
    o_b = o_b.reshape(S, B * S5_CHANNELS)

    k_cmp, v_cmp = _compress_call(kvc, p["cmp_pe"], p["cmp_w1"], p["cmp_b1"], p["cmp_w2"], p["cmp_b2"],
                                  p["c_k_gain"][0])
    o_cmp, sel = _cmp_select_call(qc, k_cmp, v_cmp, gates)
    o_slc = _selected_call(qc, ksel, vsel, sel, gates)
    o_win = _band_call(qc, kwin, vwin, window=C_WINDOW, gates=gates, gate_branch=2)

    x = _mix_call(o_a, o_b, o_cmp, o_slc, o_win, p["out_norm_g"][b_end:], p["w_out"].astype(BF16), x, ga1,
                  tiles["tm_mix"])
    x = _ffn_call(x, p["norm2_g"], sc2, sh2, ga2, p["w_ff1"].astype(BF16), p["w_ff2"].astype(BF16),
                  tiles["tm_ffn"], tiles["tf_ffn"])
    return x


def _tiles(S):
    return dict(tm_in=min(512, S), tc_s5=min(128, S), tm_mix=min(512, S), tm_ffn=min(1024, S), tf_ffn=512)


def kernel(x, c, norm1_g, norm2_g, w_ada, b_ada, w_in, a_q_gain, a_k_gain, a_sinks, s5_a_re, s5_a_im, s5_log_step, s5_b_re, s5_b_im, s5_c_re, s5_c_im, s5_d, s5_w_glu, s5_b_glu, c_q_gain, c_k_gain, cmp_pe, cmp_w1, cmp_b1, cmp_w2, cmp_b2, out_norm_g, w_out, w_ff1, w_ff2):
    per_layer = dict(norm1_g=norm1_g, norm2_g=norm2_g, w_in=w_in, a_q_gain=a_q_gain, a_k_gain=a_k_gain,
                     a_sinks=a_sinks, s5_a_re=s5_a_re, s5_a_im=s5_a_im, s5_log_step=s5_log_step,
                     s5_b_re=s5_b_re, s5_b_im=s5_b_im, s5_c_re=s5_c_re, s5_c_im=s5_c_im, s5_d=s5_d,
                     s5_w_glu=s5_w_glu, s5_b_glu=s5_b_glu, c_q_gain=c_q_gain, c_k_gain=c_k_gain,
                     cmp_pe=cmp_pe, cmp_w1=cmp_w1, cmp_b1=cmp_b1, cmp_w2=cmp_w2, cmp_b2=cmp_b2,
                     out_norm_g=out_norm_g, w_out=w_out, w_ff1=w_ff1, w_ff2=w_ff2)
    depth = w_in.shape[0]
    mod = _ada_call(c, w_ada, b_ada)
    tiles = _tiles(x.shape[1])
    for l in range(depth):
        x = _layer(x, mod[l], {k: v[l] for k, v in per_layer.items()}, tiles)
    return x
```

```python
import functools
import math

import numpy as np
import jax
import jax.numpy as jnp
from jax import lax
from jax.experimental import pallas as pl
from jax.experimental.pallas import tpu as pltpu

F32 = jnp.float32
BF16 = jnp.bfloat16

HEAD_DIM = 64
KV_HEADS = 2
GROUP = 3
Q_HEADS = KV_HEADS * GROUP
ATT_WIDTH = Q_HEADS * HEAD_DIM
A_WINDOW = 128
C_WINDOW = 512
S5_GROUP = 16
S5_GROUPS = 16
S5_CHANNELS = S5_GROUP * S5_GROUPS
S5_STATE = 64
S5_WIDTH = S5_GROUPS * S5_STATE
S5_MIN_NEG = -1e-4
N_BRANCH = 3
CMP_BLOCK = 32
CMP_STRIDE = 16
CMP_HIDDEN = 256
SEL_BLOCK = 64
SEL_TOPK = 16
FORCE_BONUS = 1e4
NEG_INF = -1e30
UNSELECTED_BIAS = -32768.0
EPS = 1e-6
LANES = 128
V7X_VMEM_BYTES = 64 * 1024 * 1024
VMEM_LIMIT = V7X_VMEM_BYTES - 8 * 1024 * 1024

SEL_TILE = 256
NORM_WIDTH = 2 * ATT_WIDTH + 3 * LANES
IN_COLS = NORM_WIDTH + 8 * LANES


def _params(*semantics):
    return pltpu.CompilerParams(dimension_semantics=semantics, vmem_limit_bytes=VMEM_LIMIT)


def _dot(a, b):
    return jnp.dot(a, b, preferred_element_type=F32)


def _dot_nt(a, b):
    return lax.dot_general(a, b, (((1,), (1,)), ((), ())), preferred_element_type=F32)


def _eye(n):
    r = lax.broadcasted_iota(jnp.int32, (n, n), 0)
    c = lax.broadcasted_iota(jnp.int32, (n, n), 1)
    return jnp.where(r == c, 1.0, 0.0).astype(BF16)


def _split_bf16(x):
    hi = x.astype(BF16)
    lo = (x - hi.astype(F32)).astype(BF16)
    return hi, lo


def _ada_kernel(c_ref, w_ref, b_ref, o_ref):
    c = c_ref[...]
    a = c * jax.nn.sigmoid(c)
    a_hi, a_lo = _split_bf16(a)
    w_hi, w_lo = _split_bf16(w_ref[...])
    acc = _dot(a_hi, w_hi) + _dot(a_lo, w_hi) + _dot(a_hi, w_lo)
    o_ref[...] = acc + b_ref[...]


def _ada_call(c, w_ada, b_ada):
    L, D, N = w_ada.shape
    B = c.shape[0]
    tn = 768
    return pl.pallas_call(
        _ada_kernel,
        grid=(L, N // tn),
        in_specs=[pl.BlockSpec((B, D), lambda l, j: (0, 0)),
                  pl.BlockSpec((None, D, tn), lambda l, j: (l, 0, j)),
                  pl.BlockSpec((None, 1, tn), lambda l, j: (l, 0, j))],
        out_specs=pl.BlockSpec((None, B, tn), lambda l, j: (l, 0, j)),
        out_shape=jax.ShapeDtypeStruct((L, B, N), F32),
        compiler_params=_params("arbitrary", "arbitrary"),
        name="adaln",
    )(c, w_ada, b_ada.reshape(L, 1, N))


def _inproj_kernel(x_ref, g_ref, sc_ref, sh_ref, w_ref, j_ref, qkg_ref,
                   qa_ref, qc_ref, ka_ref, ksel_ref, kwin_ref, va_ref, vsel_ref, vwin_ref,
                   kvc_ref, su_ref, gate_ref):
    x = x_ref[...]
    ms = jnp.mean(x * x, axis=-1, keepdims=True)
    h = x * lax.rsqrt(ms + EPS) * g_ref[...]
    h = h * (1.0 + sc_ref[...]) + sh_ref[...]
    proj = _dot(h.astype(BF16), w_ref[...])

    j2 = j_ref[...]
    normed = []
    for c0 in range(0, NORM_WIDTH, 2 * LANES):
        w = min(2 * LANES, NORM_WIDTH - c0)
        z = proj[:, c0:c0 + w]
        msq = _dot((z * z).astype(BF16), j2[:w, :w])
        normed.append(z * lax.rsqrt(msq + EPS) * qkg_ref[:, c0:c0 + w])
    zn = jnp.concatenate(normed, axis=-1).astype(BF16)

    def heads(dst_ref, src, col0, n):
        for i in range(n):
            dst_ref[i] = src[:, col0 + i * HEAD_DIM: col0 + (i + 1) * HEAD_DIM]

    heads(qa_ref, zn, 0, Q_HEADS)
    heads(qc_ref, zn, ATT_WIDTH, Q_HEADS)
    heads(ka_ref, zn, 2 * ATT_WIDTH, KV_HEADS)
    tm = x.shape[0]
    pos = pl.program_id(1) * tm + lax.broadcasted_iota(jnp.int32, (tm, HEAD_DIM), 0)
    blk = lax.broadcasted_iota(jnp.int32, (tm, HEAD_DIM), 1)
    onehot = jnp.where((pos >> (SEL_BLOCK.bit_length() - 1)) == blk, 1.0, 0.0).astype(BF16)
    c = 2 * ATT_WIDTH + LANES
    for i in range(KV_HEADS):
        ksel_ref[i] = jnp.concatenate([zn[:, c + i * HEAD_DIM: c + (i + 1) * HEAD_DIM], onehot], axis=-1)
    heads(kwin_ref, zn, 2 * ATT_WIDTH + 2 * LANES, KV_HEADS)
    c = NORM_WIDTH
    vb = proj[:, c:c + 3 * LANES].astype(BF16)
    heads(va_ref, vb, 0, KV_HEADS)
    eye = _eye(HEAD_DIM)
    for i in range(KV_HEADS):
        vt = _dot_nt(eye, vb[:, LANES + i * HEAD_DIM: LANES + (i + 1) * HEAD_DIM]).astype(BF16)
        for t in range(tm // SEL_TILE):
            vsel_ref[i, t] = vt[:, t * SEL_TILE:(t + 1) * SEL_TILE]
    heads(vwin_ref, vb, 2 * LANES, KV_HEADS)
    c += 3 * LANES
    heads(kvc_ref, proj, c, 2 * KV_HEADS)
    c += 2 * LANES
    su_ref[...] = proj[:, c:c + S5_CHANNELS]
    c += S5_CHANNELS
    gate_ref[...] = jax.nn.sigmoid(proj[:, c:c + LANES])


def _inproj_weights(w_in, a_q_gain, a_k_gain, c_q_gain, c_k_gain):
    D = w_in.shape[0]
    o = np.cumsum([0, ATT_WIDTH, LANES, LANES, S5_CHANNELS, ATT_WIDTH, 3 * LANES, 3 * LANES, Q_HEADS * N_BRANCH])
    aq, ak, av, su, cq, ck, cv, cg = [w_in[:, o[i]:o[i + 1]] for i in range(8)]
    ck = ck.reshape(D, N_BRANCH, LANES)
    cv = cv.reshape(D, N_BRANCH, LANES)
    gpad = jnp.zeros((D, LANES - cg.shape[1]), w_in.dtype)
    w = jnp.concatenate([aq, cq, ak, ck[:, 1], ck[:, 2],
                         av, cv[:, 1], cv[:, 2], ck[:, 0], cv[:, 0], su, cg, gpad], axis=1)
    scale = HEAD_DIM ** -0.5
    gains = jnp.concatenate([jnp.tile(a_q_gain, Q_HEADS) * scale, jnp.tile(c_q_gain, Q_HEADS) * scale,
                             jnp.tile(a_k_gain, KV_HEADS), jnp.tile(c_k_gain[1], KV_HEADS),
                             jnp.tile(c_k_gain[2], KV_HEADS)]).reshape(1, NORM_WIDTH)
    return w.astype(BF16), gains.astype(F32)


def _head_mean_matrix():
    n = 2 * LANES
    m = (np.arange(n)[:, None] // HEAD_DIM == np.arange(n)[None, :] // HEAD_DIM).astype(np.float32) / HEAD_DIM
    return jnp.asarray(m, BF16)


def _inproj_call(x, gain, sc, sh, w, qk_gain, tm):
    B, S, D = x.shape
    hs = lambda n, dt: jax.ShapeDtypeStruct((B, n, S, HEAD_DIM), dt)
    hspec = lambda n: pl.BlockSpec((None, n, tm, HEAD_DIM), lambda b, i: (b, 0, i, 0))
    vec = pl.BlockSpec((None, 1, D), lambda b, i: (b, 0, 0))
    return pl.pallas_call(
        _inproj_kernel,
        grid=(B, S // tm),
        in_specs=[pl.BlockSpec((None, tm, D), lambda b, i: (b, i, 0)),
                  pl.BlockSpec((1, D), lambda b, i: (0, 0)), vec, vec,
                  pl.BlockSpec((D, IN_COLS), lambda b, i: (0, 0)),
                  pl.BlockSpec((2 * LANES, 2 * LANES), lambda b, i: (0, 0)),
                  pl.BlockSpec((1, NORM_WIDTH), lambda b, i: (0, 0))],
        out_specs=[hspec(Q_HEADS), hspec(Q_HEADS), hspec(KV_HEADS),
                   pl.BlockSpec((None, KV_HEADS, tm, LANES), lambda b, i: (b, 0, i, 0)),
                   hspec(KV_HEADS), hspec(KV_HEADS),
                   pl.BlockSpec((None, KV_HEADS, tm // SEL_TILE, HEAD_DIM, SEL_TILE), lambda b, i: (b, 0, i, 0, 0)),
                   hspec(KV_HEADS), hspec(2 * KV_HEADS),
                   pl.BlockSpec((tm, S5_CHANNELS), lambda b, i: (i, b)),
                   pl.BlockSpec((None, tm, LANES), lambda b, i: (b, i, 0))],
        out_shape=[hs(Q_HEADS, BF16), hs(Q_HEADS, BF16), hs(KV_HEADS, BF16),
                   jax.ShapeDtypeStruct((B, KV_HEADS, S, LANES), BF16),
                   hs(KV_HEADS, BF16), hs(KV_HEADS, BF16),
                   jax.ShapeDtypeStruct((B, KV_HEADS, S // SEL_TILE, HEAD_DIM, SEL_TILE), BF16),
                   hs(KV_HEADS, BF16), hs(2 * KV_HEADS, F32),
                   jax.ShapeDtypeStruct((S, B * S5_CHANNELS), F32),
                   jax.ShapeDtypeStruct((B, S, LANES), F32)],
        compiler_params=_params("arbitrary", "arbitrary"),
        name="inproj",
    )(x, gain.reshape(1, D), sc, sh, w, _head_mean_matrix(), qk_gain)


def _band_kernel(*refs, tq, span, window, has_sink, gate_branch, has_norm):
    refs = list(refs)
    sink_ref = refs.pop(0) if has_sink else None
    q_ref, k_ref, v_ref = refs[:3]
    refs = refs[3:]
    gate_ref = refs.pop(0) if gate_branch is not None else None
    gain_ref = refs.pop(0) if has_norm else None
    o_ref = refs[0]

    q0 = pl.program_id(1) * tq
    start = pl.multiple_of(jnp.maximum(q0 + tq - span, 0), LANES)
    rows = lax.broadcasted_iota(jnp.int32, (GROUP * tq, span), 0)
    cols = lax.broadcasted_iota(jnp.int32, (GROUP * tq, span), 1)
    diff = (q0 + (rows & (tq - 1))) - (start + cols)
    mask = jnp.where(diff >= 0, diff, window) < window
    if has_sink:
        row1 = lax.broadcasted_iota(jnp.int32, (GROUP * tq, 1), 0)

    outs = []
    for h in range(KV_HEADS):
        q = q_ref[h * GROUP:(h + 1) * GROUP].reshape(GROUP * tq, HEAD_DIM)
        k = k_ref[h, pl.ds(start, span), :]
        v = v_ref[h, pl.ds(start, span), :]
        s = jnp.where(mask, _dot_nt(q, k), NEG_INF)
        m = jnp.max(s, axis=-1, keepdims=True)
        if has_sink:
            sink = jnp.zeros((GROUP * tq, 1), F32)
            for g in range(GROUP):
                sink = jnp.where(row1 >= g * tq, sink_ref[h * GROUP + g], sink)
            m = jnp.maximum(m, sink)
        e = jnp.exp(s - m)
        den = jnp.sum(e, axis=-1, keepdims=True)
        if has_sink:
            den = den + jnp.exp(sink - m)
        o = _dot(e.astype(BF16), v) / den
        for g in range(GROUP):
            og = o[g * tq:(g + 1) * tq]
            if gate_branch is not None:
                col = (h * GROUP + g) * N_BRANCH + gate_branch
                og = og * gate_ref[:, col:col + 1]
            outs.append(og)
    o_all = jnp.concatenate(outs, axis=-1)
    if has_norm:
        ms = jnp.mean(o_all * o_all, axis=-1, keepdims=True)
        o_all = o_all * lax.rsqrt(ms + EPS) * gain_ref[...]
    o_ref[...] = o_all.astype(o_ref.dtype)


def _band_call(q, k, v, *, window, sinks=None, gates=None, gate_branch=None, norm_gain=None, out_dtype=F32):
    B, _, S, _ = q.shape
    tq = LANES
    span = min(S, tq + -(-(window - 1) // LANES) * LANES)
    kern = functools.partial(_band_kernel, tq=tq, span=span, window=window, has_sink=sinks is not None,
                             gate_branch=gate_branch, has_norm=norm_gain is not None)
    in_specs, args = [], []
    if sinks is not None:
        in_specs.append(pl.BlockSpec(memory_space=pltpu.SMEM))
        args.append(sinks.astype(F32))
    in_specs += [pl.BlockSpec((None, Q_HEADS, tq, HEAD_DIM), lambda b, i: (b, 0, i, 0)),
                 pl.BlockSpec((None, KV_HEADS, S, HEAD_DIM), lambda b, i: (b, 0, 0, 0)),
                 pl.BlockSpec((None, KV_HEADS, S, HEAD_DIM), lambda b, i: (b, 0, 0, 0))]
    args += [q, k, v]
    if gates is not None:
        in_specs.append(pl.BlockSpec((None, tq, LANES), lambda b, i: (b, i, 0)))
        args.append(gates)
    if norm_gain is not None:
        in_specs.append(pl.BlockSpec((1, ATT_WIDTH), lambda b, i: (0, 0)))
        args.append(norm_gain.reshape(1, ATT_WIDTH).astype(F32))
    return pl.pallas_call(
        kern,
        grid=(B, S // tq),
        in_specs=in_specs,
        out_specs=pl.BlockSpec((None, tq, ATT_WIDTH), lambda b, i: (b, i, 0)),
        out_shape=jax.ShapeDtypeStruct((B, S, ATT_WIDTH), out_dtype),
        compiler_params=_params("arbitrary", "arbitrary"),
        name=f"band_attn_w{window}",
    )(*args)


def _s5_kernel(u_ref, bm_ref, are_ref, aim_ref, cm_ref, d_ref, wg_ref, bg_ref, gain_ref, o_ref,
               h_ref, st_ref, *, tc, nb):
    @pl.when(pl.program_id(0) == 0)
    def _():
        st_ref[...] = jnp.zeros_like(st_ref)

    u = u_ref[...]
    h_ref[...] = _dot(u.astype(BF16), bm_ref[...])
    a_re = are_ref[...]
    a_im = aim_ref[...]

    def step(t, carry):
        h_re, h_im = carry
        r = pl.ds(pl.multiple_of(t * nb, nb), nb)
        b_re = h_ref[r, 0:S5_WIDTH]
        b_im = h_ref[r, S5_WIDTH:2 * S5_WIDTH]
        n_re = a_re * h_re - a_im * h_im + b_re
        n_im = a_re * h_im + a_im * h_re + b_im
        h_ref[r, 0:S5_WIDTH] = n_re
        h_ref[r, S5_WIDTH:2 * S5_WIDTH] = n_im
        return n_re, n_im

    h_re, h_im = lax.fori_loop(0, tc, step, (st_ref[:, 0:S5_WIDTH], st_ref[:, S5_WIDTH:2 * S5_WIDTH]))
    st_ref[:, 0:S5_WIDTH] = h_re
    st_ref[:, S5_WIDTH:2 * S5_WIDTH] = h_im

    y = _dot(h_ref[...].astype(BF16), cm_ref[...]) + d_ref[...] * u
    hg = jax.nn.gelu(y)
    out = hg * jax.nn.sigmoid(_dot(hg.astype(BF16), wg_ref[...]) + bg_ref[...])
    ms = jnp.mean(out * out, axis=-1, keepdims=True)
    o_ref[...] = (out * lax.rsqrt(ms + EPS) * gain_ref[...]).astype(o_ref.dtype)


def _s5_weights(a_re, a_im, log_step, b_re, b_im, c_re, c_im):
    lam_re = jnp.minimum(a_re, S5_MIN_NEG)
    lam_im = a_im
    step = jnp.exp(log_step)[:, None]
    mag = jnp.exp(lam_re * step)
    abar_re = mag * jnp.cos(lam_im * step)
    abar_im = mag * jnp.sin(lam_im * step)
    den = lam_re * lam_re + lam_im * lam_im
    nr = abar_re - 1.0
    coef_re = (nr * lam_re + abar_im * lam_im) / den
    coef_im = (abar_im * lam_re - nr * lam_im) / den
    bb_re = coef_re[..., None] * b_re - coef_im[..., None] * b_im
    bb_im = coef_re[..., None] * b_im + coef_im[..., None] * b_re
    eye = jnp.eye(S5_GROUPS, dtype=F32)
    bd = lambda m: jnp.einsum('gnp,gh->gphn', m, eye).reshape(S5_CHANNELS, S5_WIDTH)
    bmat = jnp.concatenate([bd(bb_re), bd(bb_im)], axis=1)
    cd = lambda m: jnp.einsum('gpn,gh->gnhp', m, eye).reshape(S5_WIDTH, S5_CHANNELS)
    cmat = jnp.concatenate([cd(c_re), -cd(c_im)], axis=0)
    return bmat.astype(BF16), abar_re.reshape(1, S5_WIDTH), abar_im.reshape(1, S5_WIDTH), cmat.astype(BF16)


def _s5_call(u_tb, nb, bmat, abar_re, abar_im, cmat, d_skip, w_glu, b_glu, gain, tc):
    rows = u_tb.shape[0]
    S = rows // nb
    C = S5_CHANNELS
    const = lambda shape: pl.BlockSpec(shape, lambda i: (0,) * len(shape))
    return pl.pallas_call(
        functools.partial(_s5_kernel, tc=tc, nb=nb),
        grid=(S // tc,),
        in_specs=[pl.BlockSpec((tc * nb, C), lambda i: (i, 0)),
                  const((C, 2 * S5_WIDTH)), const((nb, S5_WIDTH)), const((nb, S5_WIDTH)),
                  const((2 * S5_WIDTH, C)), const((1, C)), const((C, C)), const((1, C)), const((1, C))],
        out_specs=pl.BlockSpec((tc * nb, C), lambda i: (i, 0)),
        out_shape=jax.ShapeDtypeStruct((rows, C), BF16),
        scratch_shapes=[pltpu.VMEM((tc * nb, 2 * S5_WIDTH), F32), pltpu.VMEM((nb, 2 * S5_WIDTH), F32)],
        compiler_params=_params("arbitrary"),
        name="s5_mixer",
    )(u_tb, bmat, jnp.broadcast_to(abar_re, (nb, S5_WIDTH)), jnp.broadcast_to(abar_im, (nb, S5_WIDTH)),
      cmat, d_skip.reshape(1, C), w_glu.astype(BF16), b_glu.reshape(1, C), gain.reshape(1, C))


def _compress_kernel(x_ref, pe_ref, w1_ref, b1_ref, w2_ref, b2_ref, kg_ref, k_ref, v_ref):
    half = CMP_STRIDE * HEAD_DIM
    n = x_ref.shape[1]
    for seg in range(2 * KV_HEADS):
        kind = seg // KV_HEADS
        x = x_ref[seg]
        z0 = _dot((x + pe_ref[kind, 0:1]).astype(BF16), w1_ref[kind, 0:half])
        z1 = _dot((x + pe_ref[kind, 1:2]).astype(BF16), w1_ref[kind, half:2 * half])
        z = z0 + pltpu.roll(z1, n - 1, 0)
        act = jax.nn.gelu(z + b1_ref[kind])
        out = _dot(act.astype(BF16), w2_ref[kind]) + b2_ref[kind]
        if kind == 0:
            ms = jnp.mean(out * out, axis=-1, keepdims=True)
            k_ref[seg] = (out * lax.rsqrt(ms + EPS) * kg_ref[...]).astype(BF16)
        else:
            v_ref[seg - KV_HEADS] = _dot_nt(_eye(HEAD_DIM), out.astype(BF16)).astype(BF16)


def _compress_call(kvc, pe, w1, b1, w2, b2, k_gain0):
    B, _, S, _ = kvc.shape
    n = S // CMP_STRIDE
    half = CMP_STRIDE * HEAD_DIM
    x = kvc.reshape(B, 2 * KV_HEADS, n, half)
    const = lambda shape: pl.BlockSpec(shape, lambda b: (0,) * len(shape))
    out = jax.ShapeDtypeStruct((B, KV_HEADS, n, HEAD_DIM), BF16)
    return pl.pallas_call(
        _compress_kernel,
        grid=(B,),
        in_specs=[pl.BlockSpec((None, 2 * KV_HEADS, n, half), lambda b: (b, 0, 0, 0)),
                  const((2, 2, half)), const((2, 2 * half, CMP_HIDDEN)), const((2, 1, CMP_HIDDEN)),
                  const((2, CMP_HIDDEN, HEAD_DIM)), const((2, 1, HEAD_DIM)), const((1, HEAD_DIM))],
        out_specs=[pl.BlockSpec((None, KV_HEADS, n, HEAD_DIM), lambda b: (b, 0, 0, 0)),
                   pl.BlockSpec((None, KV_HEADS, HEAD_DIM, n), lambda b: (b, 0, 0, 0))],
        out_shape=[out, jax.ShapeDtypeStruct((B, KV_HEADS, HEAD_DIM, n), BF16)],
        compiler_params=_params("arbitrary"),
        name="nsa_compress",
    )(x, pe.reshape(2, 2, half), w1.astype(BF16), b1.reshape(2, 1, CMP_HIDDEN), w2.astype(BF16),
      b2.reshape(2, 1, HEAD_DIM), k_gain0.reshape(1, HEAD_DIM))


def _cmp_select_kernel(q_ref, k_ref, vt_ref, cov_ref, gate_ref, o_ref, qa_ref, *, tq, topk):
    n_cmp = k_ref.shape[1]
    n_sel = cov_ref.shape[0]
    q0 = pl.program_id(1) * tq
    cidx = lax.broadcasted_iota(jnp.int32, (n_cmp, GROUP * tq), 0)
    t_col = q0 + (lax.broadcasted_iota(jnp.int32, (n_cmp, GROUP * tq), 1) & (tq - 1))
    ok = cidx * CMP_STRIDE + (CMP_BLOCK - 1) <= t_col
    t_row = q0 + (lax.broadcasted_iota(jnp.int32, (1, GROUP * tq), 1) & (tq - 1))
    any_ok = jnp.where(t_row >= CMP_BLOCK - 1, 1.0, 0.0)

    j = lax.broadcasted_iota(jnp.int32, (n_sel, tq), 0)
    qblk = (q0 + lax.broadcasted_iota(jnp.int32, (n_sel, tq), 1)) >> (SEL_BLOCK.bit_length() - 1)
    forced = (j == 0) | (j == qblk) | (j == qblk - 1)
    eye = _eye(HEAD_DIM)

    outs = []
    for h in range(KV_HEADS):
        q = q_ref[h * GROUP:(h + 1) * GROUP].reshape(GROUP * tq, HEAD_DIM)
        qt = _dot_nt(eye, q).astype(BF16)
        s = jnp.where(ok, _dot(k_ref[h], qt), NEG_INF)
        m = jnp.max(s, axis=0, keepdims=True)
        e = jnp.exp(s - m)
        p = e * (any_ok / jnp.sum(e, axis=0, keepdims=True))
        ot = _dot(vt_ref[h], p.astype(BF16))
        for g in range(GROUP):
            col = (h * GROUP + g) * N_BRANCH
            outs.append(ot[:, g * tq:(g + 1) * tq].T * gate_ref[:, col:col + 1])

        psum = p[:, 0:tq] + p[:, tq:2 * tq] + p[:, 2 * tq:3 * tq]
        p_hi, p_lo = _split_bf16(psum)
        imp = _dot(cov_ref[...], p_hi) + _dot(cov_ref[...], p_lo)
        imp = jnp.where(forced, imp + FORCE_BONUS, imp)
        imp = jnp.where(j <= qblk, imp, NEG_INF)
        rank = jnp.zeros((n_sel, tq), F32)
        for i in range(n_sel):
            r = imp[i:i + 1, :]
            rank = rank + jnp.where(j > i, jnp.where(r >= imp, 1.0, 0.0), jnp.where(r > imp, 1.0, 0.0))
        bias = jnp.where(rank < topk, 0.0, UNSELECTED_BIAS).astype(BF16)
        if n_sel < HEAD_DIM:
            bias = jnp.concatenate([bias, jnp.zeros((HEAD_DIM - n_sel, tq), BF16)], axis=0)
        qa_ref[h] = jnp.concatenate([qt, jnp.concatenate([bias] * GROUP, axis=1)], axis=0)
    o_ref[...] = jnp.concatenate(outs, axis=-1)


def _cover_matrix_t(n_cmp, n_sel):
    cs = np.arange(n_cmp)[None, :] * CMP_STRIDE
    ss = np.arange(n_sel)[:, None] * SEL_BLOCK
    cover = np.clip(np.minimum(cs + CMP_BLOCK, ss + SEL_BLOCK) - np.maximum(cs, ss), 0, None)
    return jnp.asarray(cover / CMP_BLOCK, BF16)


def _cmp_select_call(q, k_cmp, v_cmp_t, gates):
    B, _, S, _ = q.shape
    tq = SEL_TILE
    n_cmp = k_cmp.shape[2]
    n_sel = S // SEL_BLOCK
    assert n_sel <= HEAD_DIM
    topk = min(SEL_TOPK, n_sel)
    return pl.pallas_call(
        functools.partial(_cmp_select_kernel, tq=tq, topk=topk),
        grid=(B, S // tq),
        in_specs=[pl.BlockSpec((None, Q_HEADS, tq, HEAD_DIM), lambda b, i: (b, 0, i, 0)),
                  pl.BlockSpec((None, KV_HEADS, n_cmp, HEAD_DIM), lambda b, i: (b, 0, 0, 0)),
                  pl.BlockSpec((None, KV_HEADS, HEAD_DIM, n_cmp), lambda b, i: (b, 0, 0, 0)),
                  pl.BlockSpec((n_sel, n_cmp), lambda b, i: (0, 0)),
                  pl.BlockSpec((None, tq, LANES), lambda b, i: (b, i, 0))],
        out_specs=[pl.BlockSpec((None, tq, ATT_WIDTH), lambda b, i: (b, i, 0)),
                   pl.BlockSpec((None, KV_HEADS, None, 2 * HEAD_DIM, GROUP * tq), lambda b, i: (b, 0, i, 0, 0))],
        out_shape=[jax.ShapeDtypeStruct((B, S, ATT_WIDTH), F32),
                   jax.ShapeDtypeStruct((B, KV_HEADS, S // tq, 2 * HEAD_DIM, GROUP * tq), BF16)],
        compiler_params=_params("arbitrary", "arbitrary"),
        name="nsa_cmp_select",
    )(q, k_cmp, v_cmp_t, _cover_matrix_t(n_cmp, n_sel), gates)


def _selected_kernel(qa_ref, k_ref, vt_ref, gate_ref, o_ref, m_ref, l_ref, acc_ref, *, tq):
    i = pl.program_id(1)
    width = GROUP * tq
    krow = lax.broadcasted_iota(jnp.int32, (tq, width), 0)
    qcol = lax.broadcasted_iota(jnp.int32, (tq, width), 1) & (tq - 1)
    causal = krow <= qcol

    m_ref[...] = jnp.full_like(m_ref, NEG_INF)
    l_ref[...] = jnp.zeros_like(l_ref)
    acc_ref[...] = jnp.zeros_like(acc_ref)

    def tile(kt, diagonal):
        k0 = pl.multiple_of(kt * tq, tq)
        for h in range(KV_HEADS):
            s = _dot(k_ref[h, pl.ds(k0, tq), :], qa_ref[h])
            if diagonal:
                s = jnp.where(causal, s, NEG_INF)
            m_prev = m_ref[h]
            m_new = jnp.maximum(m_prev, jnp.max(s, axis=0, keepdims=True))
            alpha = jnp.exp(m_prev - m_new)
            p = jnp.exp(s - m_new)
            l_ref[h] = alpha * l_ref[h] + jnp.sum(p, axis=0, keepdims=True)
            acc_ref[h] = alpha * acc_ref[h] + _dot(vt_ref[h, kt], p.astype(BF16))
            m_ref[h] = m_new

    def body(kt, carry):
        tile(kt, False)
        return carry

    lax.fori_loop(0, i, body, 0)
    tile(i, True)

    outs = []
    for h in range(KV_HEADS):
        ot = acc_ref[h] / l_ref[h]
        for g in range(GROUP):
            col = (h * GROUP + g) * N_BRANCH + 1
            outs.append(ot[:, g * tq:(g + 1) * tq].T * gate_ref[:, col:col + 1])
    o_ref[...] = jnp.concatenate(outs, axis=-1)


def _selected_call(q_aug, k_aug, v_t, gates):
    B, _, n_tiles, _, width = q_aug.shape
    tq = width // GROUP
    S = n_tiles * tq
    return pl.pallas_call(
        functools.partial(_selected_kernel, tq=tq),
        grid=(B, n_tiles),
        in_specs=[pl.BlockSpec((None, KV_HEADS, None, 2 * HEAD_DIM, width), lambda b, i: (b, 0, i, 0, 0)),
                  pl.BlockSpec((None, KV_HEADS, S, 2 * HEAD_DIM), lambda b, i: (b, 0, 0, 0)),
                  pl.BlockSpec((None, KV_HEADS, n_tiles, HEAD_DIM, tq), lambda b, i: (b, 0, 0, 0, 0)),
                  pl.BlockSpec((None, tq, LANES), lambda b, i: (b, i, 0))],
        out_specs=pl.BlockSpec((None, tq, ATT_WIDTH), lambda b, i: (b, i, 0)),
        out_shape=jax.ShapeDtypeStruct((B, S, ATT_WIDTH), F32),
        scratch_shapes=[pltpu.VMEM((KV_HEADS, 1, width), F32), pltpu.VMEM((KV_HEADS, 1, width), F32),
                        pltpu.VMEM((KV_HEADS, HEAD_DIM, width), F32)],
        compiler_params=_params("arbitrary", "arbitrary"),
        name="nsa_selected",
    )(q_aug, k_aug, v_t, gates)


def _mix_kernel(oa_ref, ob_ref, c1_ref, c2_ref, c3_ref, gc_ref, w_ref, x_ref, ga_ref, o_ref):
    oc = c1_ref[...] + c2_ref[...] + c3_ref[...]
    ms = jnp.mean(oc * oc, axis=-1, keepdims=True)
    oc = (oc * lax.rsqrt(ms + EPS) * gc_ref[...]).astype(BF16)
    a_end = ATT_WIDTH
    b_end = ATT_WIDTH + S5_CHANNELS
    y = (_dot(oa_ref[...], w_ref[0:a_end]) + _dot(ob_ref[...], w_ref[a_end:b_end])
         + _dot(oc, w_ref[b_end:]))
    o_ref[...] = x_ref[...] + ga_ref[...] * y


def _mix_call(o_a, o_b_tb, o_cmp, o_slc, o_win, gain_c, w_out, x, ga, tm):
    B, S, D = x.shape
    att = pl.BlockSpec((None, tm, ATT_WIDTH), lambda b, i: (b, i, 0))
    return pl.pallas_call(
        _mix_kernel,
        grid=(B, S // tm),
        in_specs=[att, pl.BlockSpec((tm, S5_CHANNELS), lambda b, i: (i, b)), att, att, att,
                  pl.BlockSpec((1, ATT_WIDTH), lambda b, i: (0, 0)),
                  pl.BlockSpec(w_out.shape, lambda b, i: (0, 0)),
                  pl.BlockSpec((None, tm, D), lambda b, i: (b, i, 0)),
                  pl.BlockSpec((None, 1, D), lambda b, i: (b, 0, 0))],
        out_specs=pl.BlockSpec((None, tm, D), lambda b, i: (b, i, 0)),
        out_shape=jax.ShapeDtypeStruct((B, S, D), F32),
        compiler_params=_params("arbitrary", "arbitrary"),
        name="mix_outproj",
    )(o_a, o_b_tb, o_cmp, o_slc, o_win, gain_c.reshape(1, ATT_WIDTH), w_out, x, ga)


def _ffn_kernel(x_ref, g_ref, sc_ref, sh_ref, ga_ref, w1_ref, w2_ref, o_ref, h_ref, acc_ref):
    j = pl.program_id(2)

    @pl.when(j == 0)
    def _():
        x = x_ref[...]
        ms = jnp.mean(x * x, axis=-1, keepdims=True)
        h = x * lax.rsqrt(ms + EPS) * g_ref[...]
        h_ref[...] = (h * (1.0 + sc_ref[...]) + sh_ref[...]).astype(BF16)
        acc_ref[...] = jnp.zeros_like(acc_ref)

    a = jnp.maximum(_dot(h_ref[...], w1_ref[...]), 0.0)
    acc_ref[...] += _dot((a * a).astype(BF16), w2_ref[...])

    @pl.when(j == pl.num_programs(2) - 1)
    def _():
        o_ref[...] = x_ref[...] + ga_ref[...] * acc_ref[...]


def _ffn_call(x, gain, sc, sh, ga, w1, w2, tm, tf):
    B, S, D = x.shape
    F = w1.shape[1]
    vec = pl.BlockSpec((None, 1, D), lambda b, i, j: (b, 0, 0))
    return pl.pallas_call(
        _ffn_kernel,
        grid=(B, S // tm, F // tf),
        in_specs=[pl.BlockSpec((None, tm, D), lambda b, i, j: (b, i, 0)),
                  pl.BlockSpec((1, D), lambda b, i, j: (0, 0)), vec, vec, vec,
                  pl.BlockSpec((D, tf), lambda b, i, j: (0, j)),
                  pl.BlockSpec((tf, D), lambda b, i, j: (j, 0))],
        out_specs=pl.BlockSpec((None, tm, D), lambda b, i, j: (b, i, 0)),
        out_shape=jax.ShapeDtypeStruct((B, S, D), F32),
        scratch_shapes=[pltpu.VMEM((tm, D), BF16), pltpu.VMEM((tm, D), F32)],
        compiler_params=_params("arbitrary", "arbitrary", "arbitrary"),
        name="ffn",
    )(x, gain.reshape(1, D), sc, sh, ga, w1, w2)


def _layer(x, mod, p, tiles):
    B, S, D = x.shape
    sh1, sc1, ga1, sh2, sc2, ga2 = [m.reshape(B, 1, D) for m in jnp.split(mod, 6, axis=-1)]

    w_in, qk_gain = _inproj_weights(p["w_in"], p["a_q_gain"], p["a_k_gain"], p["c_q_gain"], p["c_k_gain"])
    (qa, qc, ka, ksel, kwin, va, vsel, vwin, kvc, su, gates) = _inproj_call(
        x, p["norm1_g"], sc1, sh1, w_in, qk_gain, tiles["tm_in"])

    a_end = ATT_WIDTH
    b_end = ATT_WIDTH + S5_CHANNELS
    o_a = _band_call(qa, ka, va, window=A_WINDOW, sinks=p["a_sinks"], norm_gain=p["out_norm_g"][:a_end],
                     out_dtype=BF16)

    bmat, abar_re, abar_im, cmat = _s5_weights(p["s5_a_re"], p["s5_a_im"], p["s5_log_step"],
                                               p["s5_b_re"], p["s5_b_im"], p["s5_c_re"], p["s5_c_im"])
    o_b = _s5_call(su.reshape(S * B, S5_CHANNELS), B, bmat, abar_re, abar_im, cmat, p["s5_d"],
                   p["s5_w_glu"], p["s5_b_glu"], p["out_norm_g"][a_end:b_end], tiles["tc_s5"])
    o_b = o_b.reshape(S, B * S5_CHANNELS)

    k_cmp, v_cmp = _compress_call(kvc, p["cmp_pe"], p["cmp_w1"], p["cmp_b1"], p["cmp_w2"], p["cmp_b2"],
                                  p["c_k_gain"][0])
    o_cmp, q_aug = _cmp_select_call(qc, k_cmp, v_cmp, gates)
    o_slc = _selected_call(q_aug, ksel, vsel, gates)
    o_win = _band_call(qc, kwin, vwin, window=C_WINDOW, gates=gates, gate_branch=2)

    x = _mix_call(o_a, o_b, o_cmp, o_slc, o_win, p["out_norm_g"][b_end:], p["w_out"].astype(BF16), x, ga1,
                  tiles["tm_mix"])
    x = _ffn_call(x, p["norm2_g"], sc2, sh2, ga2, p["w_ff1"].astype(BF16), p["w_ff2"].astype(BF16),
                  tiles["tm_ffn"], tiles["tf_ffn"])
    return x


def _tiles(S):
    return dict(tm_in=min(512, S), tc_s5=min(128, S), tm_mix=min(512, S), tm_ffn=min(1024, S), tf_ffn=512)


def kernel(x, c, norm1_g, norm2_g, w_ada, b_ada, w_in, a_q_gain, a_k_gain, a_sinks, s5_a_re, s5_a_im, s5_log_step, s5_b_re, s5_b_im, s5_c_re, s5_c_im, s5_d, s5_w_glu, s5_b_glu, c_q_gain, c_k_gain, cmp_pe, cmp_w1, cmp_b1, cmp_w2, cmp_b2, out_norm_g, w_out, w_ff1, w_ff2):
    per_layer = dict(norm1_g=norm1_g, norm2_g=norm2_g, w_in=w_in, a_q_gain=a_q_gain, a_k_gain=a_k_gain,
                     a_sinks=a_sinks, s5_a_re=s5_a_re, s5_a_im=s5_a_im, s5_log_step=s5_log_step,
                     s5_b_re=s5_b_re, s5_b_im=s5_b_im, s5_c_re=s5_c_re, s5_c_im=s5_c_im, s5_d=s5_d,
                     s5_w_glu=s5_w_glu, s5_b_glu=s5_b_glu, c_q_gain=c_q_gain, c_k_gain=c_k_gain,
                     cmp_pe=cmp_pe, cmp_w1=cmp_w1, cmp_b1=cmp_b1, cmp_w2=cmp_w2, cmp_b2=cmp_b2,
                     out_norm_g=out_norm_g, w_out=w_out, w_ff1=w_ff1, w_ff2=w_ff2)
    depth = w_in.shape[0]
    mod = _ada_call(c, w_ada, b_ada)
    tiles = _tiles(x.shape[1])
    for l in range(depth):
        x = _layer(x, mod[l], {k: v[l] for k, v in per_layer.items()}, tiles)
    return x
```

```python
import functools
import math

import numpy as np
import jax
import jax.numpy as jnp
from jax import lax
from jax.experimental import pallas as pl
from jax.experimental.pallas import tpu as pltpu

F32 = jnp.float32
BF16 = jnp.bfloat16

HEAD_DIM = 64
KV_HEADS = 2
GROUP = 3
Q_HEADS = KV_HEADS * GROUP
ATT_WIDTH = Q_HEADS * HEAD_DIM
A_WINDOW = 128
C_WINDOW = 512
S5_GROUP = 16
S5_GROUPS = 16
S5_CHANNELS = S5_GROUP * S5_GROUPS
S5_STATE = 64
S5_WIDTH = S5_GROUPS * S5_STATE
S5_MIN_NEG = -1e-4
N_BRANCH = 3
CMP_BLOCK = 32
CMP_STRIDE = 16
CMP_HIDDEN = 256
SEL_BLOCK = 64
SEL_TOPK = 16
FORCE_BONUS = 1e4
NEG_INF = -1e30
UNSELECTED_BIAS = -32768.0
EPS = 1e-6
LANES = 128
V7X_VMEM_BYTES = 64 * 1024 * 1024
VMEM_LIMIT = V7X_VMEM_BYTES - 8 * 1024 * 1024

SEL_TILE = 256
NORM_WIDTH = 2 * ATT_WIDTH + 3 * LANES
IN_COLS = NORM_WIDTH + 8 * LANES


def _params(*semantics):
    return pltpu.CompilerParams(dimension_semantics=semantics, vmem_limit_bytes=VMEM_LIMIT)


def _dot(a, b):
    return jnp.dot(a, b, preferred_element_type=F32)


def _dot_nt(a, b):
    return lax.dot_general(a, b, (((1,), (1,)), ((), ())), preferred_element_type=F32)


def _eye(n):
    r = lax.broadcasted_iota(jnp.int32, (n, n), 0)
    c = lax.broadcasted_iota(jnp.int32, (n, n), 1)
    return jnp.where(r == c, 1.0, 0.0).astype(BF16)


def _split_bf16(x):
    hi = x.astype(BF16)
    lo = (x - hi.astype(F32)).astype(BF16)
    return hi, lo


def _transpose_exact(x):
    eye = _eye(x.shape[1])
    hi = x.astype(BF16)
    mid, lo = _split_bf16(x - hi.astype(F32))
    return _dot_nt(eye, hi) + _dot_nt(eye, mid) + _dot_nt(eye, lo)


def _ada_kernel(c_ref, w_ref, b_ref, o_ref):
    c = c_ref[...]
    a = c * jax.nn.sigmoid(c)
    a_hi, a_lo = _split_bf16(a)
    w_hi, w_lo = _split_bf16(w_ref[...])
    acc = _dot(a_hi, w_hi) + _dot(a_lo, w_hi) + _dot(a_hi, w_lo)
    o_ref[...] = acc + b_ref[...]


def _ada_call(c, w_ada, b_ada):
    L, D, N = w_ada.shape
    B = c.shape[0]
    tn = 768
    return pl.pallas_call(
        _ada_kernel,
        grid=(L, N // tn),
        in_specs=[pl.BlockSpec((B, D), lambda l, j: (0, 0)),
                  pl.BlockSpec((None, D, tn), lambda l, j: (l, 0, j)),
                  pl.BlockSpec((None, 1, tn), lambda l, j: (l, 0, j))],
        out_specs=pl.BlockSpec((None, B, tn), lambda l, j: (l, 0, j)),
        out_shape=jax.ShapeDtypeStruct((L, B, N), F32),
        compiler_params=_params("arbitrary", "arbitrary"),
        name="adaln",
    )(c, w_ada, b_ada.reshape(L, 1, N))


def _inproj_kernel(x_ref, g_ref, sc_ref, sh_ref, w_ref, j_ref, qkg_ref,
                   qa_ref, qc_ref, ka_ref, ksel_ref, kwin_ref, va_ref, vsel_ref, vwin_ref,
                   kvc_ref, su_ref, gate_ref):
    x = x_ref[...]
    ms = jnp.mean(x * x, axis=-1, keepdims=True)
    h = x * lax.rsqrt(ms + EPS) * g_ref[...]
    h = h * (1.0 + sc_ref[...]) + sh_ref[...]
    proj = _dot(h.astype(BF16), w_ref[...])

    j2 = j_ref[...]
    normed = []
    for c0 in range(0, NORM_WIDTH, 2 * LANES):
        w = min(2 * LANES, NORM_WIDTH - c0)
        z = proj[:, c0:c0 + w]
        msq = _dot((z * z).astype(BF16), j2[:w, :w])
        normed.append(z * lax.rsqrt(msq + EPS) * qkg_ref[:, c0:c0 + w])
    zn = jnp.concatenate(normed, axis=-1).astype(BF16)

    def heads(dst_ref, src, col0, n):
        for i in range(n):
            dst_ref[i] = src[:, col0 + i * HEAD_DIM: col0 + (i + 1) * HEAD_DIM]

    tm = x.shape[0]
    eye = _eye(HEAD_DIM)

    def head_cols(src, col0, i):
        return src[:, col0 + i * HEAD_DIM: col0 + (i + 1) * HEAD_DIM]

    def transposed_tiles(dst_ref, h, cols):
        ts = [_dot_nt(eye, c).astype(BF16) for c in cols]
        for t in range(tm // SEL_TILE):
            dst_ref[h, t] = jnp.concatenate([x[:, t * SEL_TILE:(t + 1) * SEL_TILE] for x in ts], axis=1)

    for h in range(KV_HEADS):
        transposed_tiles(qa_ref, h, [head_cols(zn, 0, h * GROUP + g) for g in range(GROUP)])
    heads(qc_ref, zn, ATT_WIDTH, Q_HEADS)
    heads(ka_ref, zn, 2 * ATT_WIDTH, KV_HEADS)
    pos = pl.program_id(1) * tm + lax.broadcasted_iota(jnp.int32, (tm, HEAD_DIM), 0)
    blk = lax.broadcasted_iota(jnp.int32, (tm, HEAD_DIM), 1)
    onehot = jnp.where((pos >> (SEL_BLOCK.bit_length() - 1)) == blk, 1.0, 0.0).astype(BF16)
    c = 2 * ATT_WIDTH + LANES
    for i in range(KV_HEADS):
        ksel_ref[i] = jnp.concatenate([zn[:, c + i * HEAD_DIM: c + (i + 1) * HEAD_DIM], onehot], axis=-1)
    heads(kwin_ref, zn, 2 * ATT_WIDTH + 2 * LANES, KV_HEADS)
    c = NORM_WIDTH
    vb = proj[:, c:c + 3 * LANES].astype(BF16)
    for h in range(KV_HEADS):
        for dst_ref, col0 in ((va_ref, 0), (vsel_ref, LANES), (vwin_ref, 2 * LANES)):
            dst_ref[h] = _dot_nt(eye, head_cols(vb, col0, h)).astype(BF16)
    c += 3 * LANES
    heads(kvc_ref, proj, c, 2 * KV_HEADS)
    c += 2 * LANES
    su_ref[...] = proj[:, c:c + S5_CHANNELS]
    c += S5_CHANNELS
    gate_ref[...] = _transpose_exact(jax.nn.sigmoid(proj[:, c:c + LANES]))


def _inproj_weights(w_in, a_q_gain, a_k_gain, c_q_gain, c_k_gain):
    D = w_in.shape[0]
    o = np.cumsum([0, ATT_WIDTH, LANES, LANES, S5_CHANNELS, ATT_WIDTH, 3 * LANES, 3 * LANES, Q_HEADS * N_BRANCH])
    aq, ak, av, su, cq, ck, cv, cg = [w_in[:, o[i]:o[i + 1]] for i in range(8)]
    ck = ck.reshape(D, N_BRANCH, LANES)
    cv = cv.reshape(D, N_BRANCH, LANES)
    gpad = jnp.zeros((D, LANES - cg.shape[1]), w_in.dtype)
    w = jnp.concatenate([aq, cq, ak, ck[:, 1], ck[:, 2],
                         av, cv[:, 1], cv[:, 2], ck[:, 0], cv[:, 0], su, cg, gpad], axis=1)
    scale = HEAD_DIM ** -0.5
    gains = jnp.concatenate([jnp.tile(a_q_gain, Q_HEADS) * scale, jnp.tile(c_q_gain, Q_HEADS) * scale,
                             jnp.tile(a_k_gain, KV_HEADS), jnp.tile(c_k_gain[1], KV_HEADS),
                             jnp.tile(c_k_gain[2], KV_HEADS)]).reshape(1, NORM_WIDTH)
    return w.astype(BF16), gains.astype(F32)


def _head_mean_matrix():
    n = 2 * LANES
    m = (np.arange(n)[:, None] // HEAD_DIM == np.arange(n)[None, :] // HEAD_DIM).astype(np.float32) / HEAD_DIM
    return jnp.asarray(m, BF16)


def _inproj_call(x, gain, sc, sh, w, qk_gain, tm):
    B, S, D = x.shape
    hs = lambda n, dt: jax.ShapeDtypeStruct((B, n, S, HEAD_DIM), dt)
    hspec = lambda n: pl.BlockSpec((None, n, tm, HEAD_DIM), lambda b, i: (b, 0, i, 0))
    ts = lambda w: jax.ShapeDtypeStruct((B, KV_HEADS, S // SEL_TILE, HEAD_DIM, w), BF16)
    tspec = lambda w: pl.BlockSpec((None, KV_HEADS, tm // SEL_TILE, HEAD_DIM, w), lambda b, i: (b, 0, i, 0, 0))
    vs = jax.ShapeDtypeStruct((B, KV_HEADS, HEAD_DIM, S), BF16)
    vspec = pl.BlockSpec((None, KV_HEADS, HEAD_DIM, tm), lambda b, i: (b, 0, 0, i))
    vec = pl.BlockSpec((None, 1, D), lambda b, i: (b, 0, 0))
    return pl.pallas_call(
        _inproj_kernel,
        grid=(B, S // tm),
        in_specs=[pl.BlockSpec((None, tm, D), lambda b, i: (b, i, 0)),
                  pl.BlockSpec((1, D), lambda b, i: (0, 0)), vec, vec,
                  pl.BlockSpec((D, IN_COLS), lambda b, i: (0, 0)),
                  pl.BlockSpec((2 * LANES, 2 * LANES), lambda b, i: (0, 0)),
                  pl.BlockSpec((1, NORM_WIDTH), lambda b, i: (0, 0))],
        out_specs=[tspec(GROUP * SEL_TILE), hspec(Q_HEADS), hspec(KV_HEADS),
                   pl.BlockSpec((None, KV_HEADS, tm, LANES), lambda b, i: (b, 0, i, 0)),
                   hspec(KV_HEADS), vspec, vspec, vspec, hspec(2 * KV_HEADS),
                   pl.BlockSpec((tm, S5_CHANNELS), lambda b, i: (i, b)),
                   pl.BlockSpec((None, LANES, tm), lambda b, i: (b, 0, i))],
        out_shape=[ts(GROUP * SEL_TILE), hs(Q_HEADS, BF16), hs(KV_HEADS, BF16),
                   jax.ShapeDtypeStruct((B, KV_HEADS, S, LANES), BF16),
                   hs(KV_HEADS, BF16), vs, vs, vs, hs(2 * KV_HEADS, F32),
                   jax.ShapeDtypeStruct((S, B * S5_CHANNELS), F32),
                   jax.ShapeDtypeStruct((B, LANES, S), F32)],
        compiler_params=_params("arbitrary", "arbitrary"),
        name="inproj",
    )(x, gain.reshape(1, D), sc, sh, w, _head_mean_matrix(), qk_gain)


def _band_kernel(*refs, tq, span, window, has_sink, gate_branch, has_norm):
    refs = list(refs)
    sink_ref = refs.pop(0) if has_sink else None
    q_ref, k_ref, v_ref = refs[:3]
    refs = refs[3:]
    gate_ref = refs.pop(0) if gate_branch is not None else None
    gain_ref = refs.pop(0) if has_norm else None
    o_ref = refs[0]

    q0 = pl.program_id(1) * tq
    start = pl.multiple_of(jnp.maximum(q0 + tq - span, 0), LANES)
    diff = (q0 - start) + (lax.broadcasted_iota(jnp.int32, (span, tq), 1)
                           - lax.broadcasted_iota(jnp.int32, (span, tq), 0))
    bias = jnp.where(jnp.where(diff >= 0, diff, window) < window, 0.0, NEG_INF)

    chains = [(h, g) for h in range(KV_HEADS) for g in range(GROUP)]
    scores = [_dot(k_ref[h, pl.ds(start, span), :], q_ref[h, :, g * tq:(g + 1) * tq]) for h, g in chains]
    probs, dens = [], []
    for (h, g), s in zip(chains, scores):
        s = s + bias
        m = jnp.max(s, axis=0, keepdims=True)
        if has_sink:
            sink = sink_ref[h * GROUP + g]
            m = jnp.maximum(m, sink)
        e = jnp.exp(s - m)
        den = jnp.sum(e, axis=0, keepdims=True)
        if has_sink:
            den = den + jnp.exp(sink - m)
        probs.append(e.astype(BF16))
        dens.append(den)
    outs = []
    for (h, g), e, den in zip(chains, probs, dens):
        scale = 1.0 / den
        if gate_branch is not None:
            col = (h * GROUP + g) * N_BRANCH + gate_branch
            scale = scale * gate_ref[col:col + 1, :]
        outs.append(_dot(v_ref[h, :, pl.ds(start, span)], e) * scale)
    o_t = jnp.concatenate(outs, axis=0)
    if has_norm:
        ms = jnp.mean(o_t * o_t, axis=0, keepdims=True)
        o_t = o_t * lax.rsqrt(ms + EPS) * gain_ref[...]
    o_ref[...] = o_t.T.astype(o_ref.dtype)


def _band_call(q, k, v, *, window, sinks=None, gates=None, gate_branch=None, norm_gain=None, out_dtype=F32):
    B, _, n_q, _, width = q.shape
    tq = width // GROUP
    S = n_q * tq
    span = min(S, tq + -(-(window - 1) // LANES) * LANES)
    kern = functools.partial(_band_kernel, tq=tq, span=span, window=window, has_sink=sinks is not None,
                             gate_branch=gate_branch, has_norm=norm_gain is not None)
    in_specs, args = [], []
    if sinks is not None:
        in_specs.append(pl.BlockSpec(memory_space=pltpu.SMEM))
        args.append(sinks.astype(F32))
    in_specs += [pl.BlockSpec((None, KV_HEADS, None, HEAD_DIM, width), lambda b, i: (b, 0, i, 0, 0)),
                 pl.BlockSpec((None, KV_HEADS, S, HEAD_DIM), lambda b, i: (b, 0, 0, 0)),
                 pl.BlockSpec((None, KV_HEADS, HEAD_DIM, S), lambda b, i: (b, 0, 0, 0))]
    args += [q, k, v]
    if gates is not None:
        in_specs.append(pl.BlockSpec((None, LANES, tq), lambda b, i: (b, 0, i)))
        args.append(gates)
    if norm_gain is not None:
        in_specs.append(pl.BlockSpec((ATT_WIDTH, tq), lambda b, i: (0, 0)))
        args.append(jnp.broadcast_to(norm_gain.astype(F32)[:, None], (ATT_WIDTH, tq)))
    return pl.pallas_call(
        kern,
        grid=(B, S // tq),
        in_specs=in_specs,
        out_specs=pl.BlockSpec((None, tq, ATT_WIDTH), lambda b, i: (b, i, 0)),
        out_shape=jax.ShapeDtypeStruct((B, S, ATT_WIDTH), out_dtype),
        compiler_params=_params("arbitrary", "arbitrary"),
        name=f"band_attn_w{window}",
    )(*args)


def _s5_kernel(u_ref, bm_ref, are_ref, aim_ref, cm_ref, d_ref, wg_ref, bg_ref, gain_ref, o_ref,
               h_ref, st_ref, *, tc, nb):
    @pl.when(pl.program_id(0) == 0)
    def _():
        st_ref[...] = jnp.zeros_like(st_ref)

    u = u_ref[...]
    h_ref[...] = _dot(u.astype(BF16), bm_ref[...])
    a_re = are_ref[...]
    a_im = aim_ref[...]

    def step(t, carry):
        h_re, h_im = carry
        r = pl.ds(pl.multiple_of(t * nb, nb), nb)
        b_re = h_ref[r, 0:S5_WIDTH]
        b_im = h_ref[r, S5_WIDTH:2 * S5_WIDTH]
        n_re = a_re * h_re - a_im * h_im + b_re
        n_im = a_re * h_im + a_im * h_re + b_im
        h_ref[r, 0:S5_WIDTH] = n_re
        h_ref[r, S5_WIDTH:2 * S5_WIDTH] = n_im
        return n_re, n_im

    h_re, h_im = lax.fori_loop(0, tc, step, (st_ref[:, 0:S5_WIDTH], st_ref[:, S5_WIDTH:2 * S5_WIDTH]))
    st_ref[:, 0:S5_WIDTH] = h_re
    st_ref[:, S5_WIDTH:2 * S5_WIDTH] = h_im

    y = _dot(h_ref[...].astype(BF16), cm_ref[...]) + d_ref[...] * u
    hg = jax.nn.gelu(y)
    out = hg * jax.nn.sigmoid(_dot(hg.astype(BF16), wg_ref[...]) + bg_ref[...])
    ms = jnp.mean(out * out, axis=-1, keepdims=True)
    o_ref[...] = (out * lax.rsqrt(ms + EPS) * gain_ref[...]).astype(o_ref.dtype)


def _s5_weights(a_re, a_im, log_step, b_re, b_im, c_re, c_im):
    lam_re = jnp.minimum(a_re, S5_MIN_NEG)
    lam_im = a_im
    step = jnp.exp(log_step)[:, None]
    mag = jnp.exp(lam_re * step)
    abar_re = mag * jnp.cos(lam_im * step)
    abar_im = mag * jnp.sin(lam_im * step)
    den = lam_re * lam_re + lam_im * lam_im
    nr = abar_re - 1.0
    coef_re = (nr * lam_re + abar_im * lam_im) / den
    coef_im = (abar_im * lam_re - nr * lam_im) / den
    bb_re = coef_re[..., None] * b_re - coef_im[..., None] * b_im
    bb_im = coef_re[..., None] * b_im + coef_im[..., None] * b_re
    eye = jnp.eye(S5_GROUPS, dtype=F32)
    bd = lambda m: jnp.einsum('gnp,gh->gphn', m, eye).reshape(S5_CHANNELS, S5_WIDTH)
    bmat = jnp.concatenate([bd(bb_re), bd(bb_im)], axis=1)
    cd = lambda m: jnp.einsum('gpn,gh->gnhp', m, eye).reshape(S5_WIDTH, S5_CHANNELS)
    cmat = jnp.concatenate([cd(c_re), -cd(c_im)], axis=0)
    return bmat.astype(BF16), abar_re.reshape(1, S5_WIDTH), abar_im.reshape(1, S5_WIDTH), cmat.astype(BF16)


def _s5_call(u_tb, nb, bmat, abar_re, abar_im, cmat, d_skip, w_glu, b_glu, gain, tc):
    rows = u_tb.shape[0]
    S = rows // nb
    C = S5_CHANNELS
    const = lambda shape: pl.BlockSpec(shape, lambda i: (0,) * len(shape))
    return pl.pallas_call(
        functools.partial(_s5_kernel, tc=tc, nb=nb),
        grid=(S // tc,),
        in_specs=[pl.BlockSpec((tc * nb, C), lambda i: (i, 0)),
                  const((C, 2 * S5_WIDTH)), const((nb, S5_WIDTH)), const((nb, S5_WIDTH)),
                  const((2 * S5_WIDTH, C)), const((1, C)), const((C, C)), const((1, C)), const((1, C))],
        out_specs=pl.BlockSpec((tc * nb, C), lambda i: (i, 0)),
        out_shape=jax.ShapeDtypeStruct((rows, C), BF16),
        scratch_shapes=[pltpu.VMEM((tc * nb, 2 * S5_WIDTH), F32), pltpu.VMEM((nb, 2 * S5_WIDTH), F32)],
        compiler_params=_params("arbitrary"),
        name="s5_mixer",
    )(u_tb, bmat, jnp.broadcast_to(abar_re, (nb, S5_WIDTH)), jnp.broadcast_to(abar_im, (nb, S5_WIDTH)),
      cmat, d_skip.reshape(1, C), w_glu.astype(BF16), b_glu.reshape(1, C), gain.reshape(1, C))


def _compress_kernel(x_ref, pe_ref, w1_ref, b1_ref, w2_ref, b2_ref, kg_ref, k_ref, v_ref):
    half = CMP_STRIDE * HEAD_DIM
    n = x_ref.shape[1]
    for seg in range(2 * KV_HEADS):
        kind = seg // KV_HEADS
        x = x_ref[seg]
        z0 = _dot((x + pe_ref[kind, 0:1]).astype(BF16), w1_ref[kind, 0:half])
        z1 = _dot((x + pe_ref[kind, 1:2]).astype(BF16), w1_ref[kind, half:2 * half])
        z = z0 + pltpu.roll(z1, n - 1, 0)
        act = jax.nn.gelu(z + b1_ref[kind])
        out = _dot(act.astype(BF16), w2_ref[kind]) + b2_ref[kind]
        if kind == 0:
            ms = jnp.mean(out * out, axis=-1, keepdims=True)
            k_ref[seg] = (out * lax.rsqrt(ms + EPS) * kg_ref[...]).astype(BF16)
        else:
            v_ref[seg - KV_HEADS] = _dot_nt(_eye(HEAD_DIM), out.astype(BF16)).astype(BF16)


def _compress_call(kvc, pe, w1, b1, w2, b2, k_gain0):
    B, _, S, _ = kvc.shape
    n = S // CMP_STRIDE
    half = CMP_STRIDE * HEAD_DIM
    x = kvc.reshape(B, 2 * KV_HEADS, n, half)
    const = lambda shape: pl.BlockSpec(shape, lambda b: (0,) * len(shape))
    out = jax.ShapeDtypeStruct((B, KV_HEADS, n, HEAD_DIM), BF16)
    return pl.pallas_call(
        _compress_kernel,
        grid=(B,),
        in_specs=[pl.BlockSpec((None, 2 * KV_HEADS, n, half), lambda b: (b, 0, 0, 0)),
                  const((2, 2, half)), const((2, 2 * half, CMP_HIDDEN)), const((2, 1, CMP_HIDDEN)),
                  const((2, CMP_HIDDEN, HEAD_DIM)), const((2, 1, HEAD_DIM)), const((1, HEAD_DIM))],
        out_specs=[pl.BlockSpec((None, KV_HEADS, n, HEAD_DIM), lambda b: (b, 0, 0, 0)),
                   pl.BlockSpec((None, KV_HEADS, HEAD_DIM, n), lambda b: (b, 0, 0, 0))],
        out_shape=[out, jax.ShapeDtypeStruct((B, KV_HEADS, HEAD_DIM, n), BF16)],
        compiler_params=_params("arbitrary"),
        name="nsa_compress",
    )(x, pe.reshape(2, 2, half), w1.astype(BF16), b1.reshape(2, 1, CMP_HIDDEN), w2.astype(BF16),
      b2.reshape(2, 1, HEAD_DIM), k_gain0.reshape(1, HEAD_DIM))


def _cmp_select_kernel(q_ref, k_ref, vt_ref, cov_ref, gate_ref, o_ref, qa_ref, *, tq, topk):
    n_cmp = k_ref.shape[1]
    n_sel = cov_ref.shape[0]
    q0 = pl.program_id(1) * tq
    cidx = lax.broadcasted_iota(jnp.int32, (n_cmp, GROUP * tq), 0)
    t_col = q0 + (lax.broadcasted_iota(jnp.int32, (n_cmp, GROUP * tq), 1) & (tq - 1))
    ok = cidx * CMP_STRIDE + (CMP_BLOCK - 1) <= t_col
    t_row = q0 + (lax.broadcasted_iota(jnp.int32, (1, GROUP * tq), 1) & (tq - 1))
    any_ok = jnp.where(t_row >= CMP_BLOCK - 1, 1.0, 0.0)

    j = lax.broadcasted_iota(jnp.int32, (n_sel, tq), 0)
    qblk = (q0 + lax.broadcasted_iota(jnp.int32, (n_sel, tq), 1)) >> (SEL_BLOCK.bit_length() - 1)
    forced = (j == 0) | (j == qblk) | (j == qblk - 1)
    eye = _eye(HEAD_DIM)

    outs = []
    for h in range(KV_HEADS):
        q = q_ref[h * GROUP:(h + 1) * GROUP].reshape(GROUP * tq, HEAD_DIM)
        qt = _dot_nt(eye, q).astype(BF16)
        s = jnp.where(ok, _dot(k_ref[h], qt), NEG_INF)
        m = jnp.max(s, axis=0, keepdims=True)
        e = jnp.exp(s - m)
        p = e * (any_ok / jnp.sum(e, axis=0, keepdims=True))
        ot = _dot(vt_ref[h], p.astype(BF16))
        for g in range(GROUP):
            col = (h * GROUP + g) * N_BRANCH
            outs.append(ot[:, g * tq:(g + 1) * tq] * gate_ref[col:col + 1, :])

        psum = p[:, 0:tq] + p[:, tq:2 * tq] + p[:, 2 * tq:3 * tq]
        p_hi, p_lo = _split_bf16(psum)
        imp = _dot(cov_ref[...], p_hi) + _dot(cov_ref[...], p_lo)
        imp = jnp.where(forced, imp + FORCE_BONUS, imp)
        imp = jnp.where(j <= qblk, imp, NEG_INF)
        rank = jnp.zeros((n_sel, tq), F32)
        for i in range(n_sel):
            r = imp[i:i + 1, :]
            rank = rank + jnp.where(j > i, jnp.where(r >= imp, 1.0, 0.0), jnp.where(r > imp, 1.0, 0.0))
        bias = jnp.where(rank < topk, 0.0, UNSELECTED_BIAS).astype(BF16)
        if n_sel < HEAD_DIM:
            bias = jnp.concatenate([bias, jnp.zeros((HEAD_DIM - n_sel, tq), BF16)], axis=0)
        qa_ref[h] = jnp.concatenate([qt, jnp.concatenate([bias] * GROUP, axis=1)], axis=0)
    o_ref[...] = jnp.concatenate(outs, axis=0).T


def _cover_matrix_t(n_cmp, n_sel):
    cs = np.arange(n_cmp)[None, :] * CMP_STRIDE
    ss = np.arange(n_sel)[:, None] * SEL_BLOCK
    cover = np.clip(np.minimum(cs + CMP_BLOCK, ss + SEL_BLOCK) - np.maximum(cs, ss), 0, None)
    return jnp.asarray(cover / CMP_BLOCK, BF16)


def _cmp_select_call(q, k_cmp, v_cmp_t, gates):
    B, _, S, _ = q.shape
    tq = SEL_TILE
    n_cmp = k_cmp.shape[2]
    n_sel = S // SEL_BLOCK
    assert n_sel <= HEAD_DIM
    topk = min(SEL_TOPK, n_sel)
    return pl.pallas_call(
        functools.partial(_cmp_select_kernel, tq=tq, topk=topk),
        grid=(B, S // tq),
        in_specs=[pl.BlockSpec((None, Q_HEADS, tq, HEAD_DIM), lambda b, i: (b, 0, i, 0)),
                  pl.BlockSpec((None, KV_HEADS, n_cmp, HEAD_DIM), lambda b, i: (b, 0, 0, 0)),
                  pl.BlockSpec((None, KV_HEADS, HEAD_DIM, n_cmp), lambda b, i: (b, 0, 0, 0)),
                  pl.BlockSpec((n_sel, n_cmp), lambda b, i: (0, 0)),
                  pl.BlockSpec((None, LANES, tq), lambda b, i: (b, 0, i))],
        out_specs=[pl.BlockSpec((None, tq, ATT_WIDTH), lambda b, i: (b, i, 0)),
                   pl.BlockSpec((None, KV_HEADS, None, 2 * HEAD_DIM, GROUP * tq), lambda b, i: (b, 0, i, 0, 0))],
        out_shape=[jax.ShapeDtypeStruct((B, S, ATT_WIDTH), F32),
                   jax.ShapeDtypeStruct((B, KV_HEADS, S // tq, 2 * HEAD_DIM, GROUP * tq), BF16)],
        compiler_params=_params("arbitrary", "arbitrary"),
        name="nsa_cmp_select",
    )(q, k_cmp, v_cmp_t, _cover_matrix_t(n_cmp, n_sel), gates)


def _selected_kernel(qa_ref, k_ref, vt_ref, gate_ref, o_ref, m_ref, l_ref, acc_ref, *, tq):
    i = pl.program_id(1)
    width = GROUP * tq
    krow = lax.broadcasted_iota(jnp.int32, (tq, width), 0)
    qcol = lax.broadcasted_iota(jnp.int32, (tq, width), 1) & (tq - 1)
    causal = krow <= qcol

    m_ref[...] = jnp.full_like(m_ref, NEG_INF)
    l_ref[...] = jnp.zeros_like(l_ref)
    acc_ref[...] = jnp.zeros_like(acc_ref)

    def scores(kt):
        k0 = pl.multiple_of(kt * tq, tq)
        return tuple(_dot(k_ref[h, pl.ds(k0, tq), :], qa_ref[h]) for h in range(KV_HEADS))

    def update(kt, s_all, diagonal):
        for h in range(KV_HEADS):
            s = s_all[h]
            if diagonal:
                s = jnp.where(causal, s, NEG_INF)
            m_prev = m_ref[h]
            m_new = jnp.maximum(m_prev, jnp.max(s, axis=0, keepdims=True))
            alpha = jnp.exp(m_prev - m_new)
            p = jnp.exp(s - m_new)
            l_ref[h] = alpha * l_ref[h] + jnp.sum(p, axis=0, keepdims=True)
            vt = vt_ref[h, :, pl.ds(pl.multiple_of(kt * tq, tq), tq)]
            acc_ref[h] = alpha * acc_ref[h] + _dot(vt, p.astype(BF16))
            m_ref[h] = m_new

    def body(kt, s_cur):
        s_next = scores(kt + 1)
        update(kt, s_cur, False)
        return s_next

    s_last = lax.fori_loop(0, i, body, scores(0))
    update(i, s_last, True)

    outs = []
    for h in range(KV_HEADS):
        ot = acc_ref[h] / l_ref[h]
        for g in range(GROUP):
            col = (h * GROUP + g) * N_BRANCH + 1
            outs.append(ot[:, g * tq:(g + 1) * tq] * gate_ref[col:col + 1, :])
    o_ref[...] = jnp.concatenate(outs, axis=0).T


def _selected_call(q_aug, k_aug, v_t, gates):
    B, _, n_tiles, _, width = q_aug.shape
    tq = width // GROUP
    S = n_tiles * tq
    return pl.pallas_call(
        functools.partial(_selected_kernel, tq=tq),
        grid=(B, n_tiles),
        in_specs=[pl.BlockSpec((None, KV_HEADS, None, 2 * HEAD_DIM, width), lambda b, i: (b, 0, i, 0, 0)),
                  pl.BlockSpec((None, KV_HEADS, S, 2 * HEAD_DIM), lambda b, i: (b, 0, 0, 0)),
                  pl.BlockSpec((None, KV_HEADS, HEAD_DIM, S), lambda b, i: (b, 0, 0, 0)),
                  pl.BlockSpec((None, LANES, tq), lambda b, i: (b, 0, i))],
        out_specs=pl.BlockSpec((None, tq, ATT_WIDTH), lambda b, i: (b, i, 0)),
        out_shape=jax.ShapeDtypeStruct((B, S, ATT_WIDTH), F32),
        scratch_shapes=[pltpu.VMEM((KV_HEADS, 1, width), F32), pltpu.VMEM((KV_HEADS, 1, width), F32),
                        pltpu.VMEM((KV_HEADS, HEAD_DIM, width), F32)],
        compiler_params=_params("arbitrary", "arbitrary"),
        name="nsa_selected",
    )(q_aug, k_aug, v_t, gates)


def _mix_kernel(oa_ref, ob_ref, c1_ref, c2_ref, c3_ref, gc_ref, w_ref, x_ref, ga_ref, o_ref):
    oc = c1_ref[...] + c2_ref[...] + c3_ref[...]
    ms = jnp.mean(oc * oc, axis=-1, keepdims=True)
    oc = (oc * lax.rsqrt(ms + EPS) * gc_ref[...]).astype(BF16)
    a_end = ATT_WIDTH
    b_end = ATT_WIDTH + S5_CHANNELS
    y = (_dot(oa_ref[...], w_ref[0:a_end]) + _dot(ob_ref[...], w_ref[a_end:b_end])
         + _dot(oc, w_ref[b_end:]))
    o_ref[...] = x_ref[...] + ga_ref[...] * y


def _mix_call(o_a, o_b_tb, o_cmp, o_slc, o_win, gain_c, w_out, x, ga, tm):
    B, S, D = x.shape
    att = pl.BlockSpec((None, tm, ATT_WIDTH), lambda b, i: (b, i, 0))
    return pl.pallas_call(
        _mix_kernel,
        grid=(B, S // tm),
        in_specs=[att, pl.BlockSpec((tm, S5_CHANNELS), lambda b, i: (i, b)), att, att, att,
                  pl.BlockSpec((1, ATT_WIDTH), lambda b, i: (0, 0)),
                  pl.BlockSpec(w_out.shape, lambda b, i: (0, 0)),
                  pl.BlockSpec((None, tm, D), lambda b, i: (b, i, 0)),
                  pl.BlockSpec((None, 1, D), lambda b, i: (b, 0, 0))],
        out_specs=pl.BlockSpec((None, tm, D), lambda b, i: (b, i, 0)),
        out_shape=jax.ShapeDtypeStruct((B, S, D), F32),
        compiler_params=_params("arbitrary", "arbitrary"),
        name="mix_outproj",
    )(o_a, o_b_tb, o_cmp, o_slc, o_win, gain_c.reshape(1, ATT_WIDTH), w_out, x, ga)


def _ffn_kernel(x_ref, g_ref, sc_ref, sh_ref, ga_ref, w1_ref, w2_ref, o_ref, h_ref, acc_ref):
    j = pl.program_id(2)

    @pl.when(j == 0)
    def _():
        x = x_ref[...]
        ms = jnp.mean(x * x, axis=-1, keepdims=True)
        h = x * lax.rsqrt(ms + EPS) * g_ref[...]
        h_ref[...] = (h * (1.0 + sc_ref[...]) + sh_ref[...]).astype(BF16)
        acc_ref[...] = jnp.zeros_like(acc_ref)

    a = jnp.maximum(_dot(h_ref[...], w1_ref[...]), 0.0)
    acc_ref[...] += _dot((a * a).astype(BF16), w2_ref[...])

    @pl.when(j == pl.num_programs(2) - 1)
    def _():
        o_ref[...] = x_ref[...] + ga_ref[...] * acc_ref[...]


def _ffn_call(x, gain, sc, sh, ga, w1, w2, tm, tf):
    B, S, D = x.shape
    F = w1.shape[1]
    vec = pl.BlockSpec((None, 1, D), lambda b, i, j: (b, 0, 0))
    return pl.pallas_call(
        _ffn_kernel,
        grid=(B, S // tm, F // tf),
        in_specs=[pl.BlockSpec((None, tm, D), lambda b, i, j: (b, i, 0)),
                  pl.BlockSpec((1, D), lambda b, i, j: (0, 0)), vec, vec, vec,
                  pl.BlockSpec((D, tf), lambda b, i, j: (0, j)),
                  pl.BlockSpec((tf, D), lambda b, i, j: (j, 0))],
        out_specs=pl.BlockSpec((None, tm, D), lambda b, i, j: (b, i, 0)),
        out_shape=jax.ShapeDtypeStruct((B, S, D), F32),
        scratch_shapes=[pltpu.VMEM((tm, D), BF16), pltpu.VMEM((tm, D), F32)],
        compiler_params=_params("arbitrary", "arbitrary", "arbitrary"),
        name="ffn",
    )(x, gain.reshape(1, D), sc, sh, ga, w1, w2)


def _layer(x, mod, p, tiles):
    B, S, D = x.shape
    sh1, sc1, ga1, sh2, sc2, ga2 = [m.reshape(B, 1, D) for m in jnp.split(mod, 6, axis=-1)]

    w_in, qk_gain = _inproj_weights(p["w_in"], p["a_q_gain"], p["a_k_gain"], p["c_q_gain"], p["c_k_gain"])
    (qa, qc, ka, ksel, kwin, va, vsel, vwin, kvc, su, gates) = _inproj_call(
        x, p["norm1_g"], sc1, sh1, w_in, qk_gain, tiles["tm_in"])

    a_end = ATT_WIDTH
    b_end = ATT_WIDTH + S5_CHANNELS
    o_a = _band_call(qa, ka, va, window=A_WINDOW, sinks=p["a_sinks"], norm_gain=p["out_norm_g"][:a_end],
                     out_dtype=BF16)

    bmat, abar_re, abar_im, cmat = _s5_weights(p["s5_a_re"], p["s5_a_im"], p["s5_log_step"],
                                               p["s5_b_re"], p["s5_b_im"], p["s5_c_re"], p["s5_c_im"])
    o_b = _s5_call(su.reshape(S * B, S5_CHANNELS), B, bmat, abar_re, abar_im, cmat, p["s5_d"],
                   p["s5_w_glu"], p["s5_b_glu"], p["out_norm_g"][a_end:b_end], tiles["tc_s5"])
    o_b = o_b.reshape(S, B * S5_CHANNELS)

    k_cmp, v_cmp = _compress_call(kvc, p["cmp_pe"], p["cmp_w1"], p["cmp_b1"], p["cmp_w2"], p["cmp_b2"],
                                  p["c_k_gain"][0])
    o_cmp, q_aug = _cmp_select_call(qc, k_cmp, v_cmp, gates)
    o_slc = _selected_call(q_aug, ksel, vsel, gates)
    o_win = _band_call(q_aug, kwin, vwin, window=C_WINDOW, gates=gates, gate_branch=2)

    x = _mix_call(o_a, o_b, o_cmp, o_slc, o_win, p["out_norm_g"][b_end:], p["w_out"].astype(BF16), x, ga1,
                  tiles["tm_mix"])
    x = _ffn_call(x, p["norm2_g"], sc2, sh2, ga2, p["w_ff1"].astype(BF16), p["w_ff2"].astype(BF16),
                  tiles["tm_ffn"], tiles["tf_ffn"])
    return x


def _tiles(S):
    return dict(tm_in=min(512, S), tc_s5=min(128, S), tm_mix=min(512, S), tm_ffn=min(1024, S), tf_ffn=512)


def kernel(x, c, norm1_g, norm2_g, w_ada, b_ada, w_in, a_q_gain, a_k_gain, a_sinks, s5_a_re, s5_a_im, s5_log_step, s5_b_re, s5_b_im, s5_c_re, s5_c_im, s5_d, s5_w_glu, s5_b_glu, c_q_gain, c_k_gain, cmp_pe, cmp_w1, cmp_b1, cmp_w2, cmp_b2, out_norm_g, w_out, w_ff1, w_ff2):
    per_layer = dict(norm1_g=norm1_g, norm2_g=norm2_g, w_in=w_in, a_q_gain=a_q_gain, a_k_gain=a_k_gain,
                     a_sinks=a_sinks, s5_a_re=s5_a_re, s5_a_im=s5_a_im, s5_log_step=s5_log_step,
                     s5_b_re=s5_b_re, s5_b_im=s5_b_im, s5_c_re=s5_c_re, s5_c_im=s5_c_im, s5_d=s5_d,
                     s5_w_glu=s5_w_glu, s5_b_glu=s5_b_glu, c_q_gain=c_q_gain, c_k_gain=c_k_gain,
                     cmp_pe=cmp_pe, cmp_w1=cmp_w1, cmp_b1=cmp_b1, cmp_w2=cmp_w2, cmp_b2=cmp_b2,
                     out_norm_g=out_norm_g, w_out=w_out, w_ff1=w_ff1, w_ff2=w_ff2)
    depth = w_in.shape[0]
    mod = _ada_call(c, w_ada, b_ada)
    tiles = _tiles(x.shape[1])
    for l in range(depth):
        x = _layer(x, mod[l], {k: v[l] for k, v in per_layer.items()}, tiles)
    return x
```

```python
import functools
import math

import numpy as np
import jax
import jax.numpy as jnp
from jax import lax
from jax.experimental import pallas as pl
from jax.experimental.pallas import tpu as pltpu

F32 = jnp.float32
BF16 = jnp.bfloat16

HEAD_DIM = 64
KV_HEADS = 2
GROUP = 3
Q_HEADS = KV_HEADS * GROUP
ATT_WIDTH = Q_HEADS * HEAD_DIM
A_WINDOW = 128
C_WINDOW = 512
S5_GROUP = 16
S5_GROUPS = 16
S5_CHANNELS = S5_GROUP * S5_GROUPS
S5_STATE = 64
S5_WIDTH = S5_GROUPS * S5_STATE
S5_MIN_NEG = -1e-4
N_BRANCH = 3
CMP_BLOCK = 32
CMP_STRIDE = 16
CMP_HIDDEN = 256
SEL_BLOCK = 64
SEL_TOPK = 16
FORCE_BONUS = 1e4
NEG_INF = -1e30
UNSELECTED_BIAS = -32768.0
EPS = 1e-6
LANES = 128
V7X_VMEM_BYTES = 64 * 1024 * 1024
VMEM_LIMIT = V7X_VMEM_BYTES - 8 * 1024 * 1024

SEL_TILE = 256
NORM_WIDTH = 2 * ATT_WIDTH + 3 * LANES
IN_COLS = NORM_WIDTH + 8 * LANES


def _params(*semantics):
    return pltpu.CompilerParams(dimension_semantics=semantics, vmem_limit_bytes=VMEM_LIMIT)


def _dot(a, b):
    return jnp.dot(a, b, preferred_element_type=F32)


def _dot_nt(a, b):
    return lax.dot_general(a, b, (((1,), (1,)), ((), ())), preferred_element_type=F32)


def _eye(n):
    r = lax.broadcasted_iota(jnp.int32, (n, n), 0)
    c = lax.broadcasted_iota(jnp.int32, (n, n), 1)
    return jnp.where(r == c, 1.0, 0.0).astype(BF16)


def _split_bf16(x):
    hi = x.astype(BF16)
    lo = (x - hi.astype(F32)).astype(BF16)
    return hi, lo


def _transpose_exact(x):
    eye = _eye(x.shape[1])
    hi = x.astype(BF16)
    mid, lo = _split_bf16(x - hi.astype(F32))
    return _dot_nt(eye, hi) + _dot_nt(eye, mid) + _dot_nt(eye, lo)


def _ada_kernel(c_ref, w_ref, b_ref, o_ref):
    c = c_ref[...]
    a = c * jax.nn.sigmoid(c)
    a_hi, a_lo = _split_bf16(a)
    w_hi, w_lo = _split_bf16(w_ref[...])
    acc = _dot(a_hi, w_hi) + _dot(a_lo, w_hi) + _dot(a_hi, w_lo)
    o_ref[...] = acc + b_ref[...]


def _ada_call(c, w_ada, b_ada):
    L, D, N = w_ada.shape
    B = c.shape[0]
    tn = 768
    return pl.pallas_call(
        _ada_kernel,
        grid=(L, N // tn),
        in_specs=[pl.BlockSpec((B, D), lambda l, j: (0, 0)),
                  pl.BlockSpec((None, D, tn), lambda l, j: (l, 0, j)),
                  pl.BlockSpec((None, 1, tn), lambda l, j: (l, 0, j))],
        out_specs=pl.BlockSpec((None, B, tn), lambda l, j: (l, 0, j)),
        out_shape=jax.ShapeDtypeStruct((L, B, N), F32),
        compiler_params=_params("arbitrary", "arbitrary"),
        name="adaln",
    )(c, w_ada, b_ada.reshape(L, 1, N))


def _inproj_kernel(x_ref, g_ref, sc_ref, sh_ref, w_ref, j_ref, qkg_ref,
                   qa_ref, qc_ref, ka_ref, ksel_ref, kwin_ref, va_ref, vsel_ref, vwin_ref,
                   kvc_ref, su_ref, gate_ref):
    x = x_ref[...]
    ms = jnp.mean(x * x, axis=-1, keepdims=True)
    h = x * lax.rsqrt(ms + EPS) * g_ref[...]
    h = h * (1.0 + sc_ref[...]) + sh_ref[...]
    proj = _dot(h.astype(BF16), w_ref[...])

    j2 = j_ref[...]
    normed = []
    for c0 in range(0, NORM_WIDTH, 2 * LANES):
        w = min(2 * LANES, NORM_WIDTH - c0)
        z = proj[:, c0:c0 + w]
        msq = _dot((z * z).astype(BF16), j2[:w, :w])
        normed.append(z * lax.rsqrt(msq + EPS) * qkg_ref[:, c0:c0 + w])
    zn = jnp.concatenate(normed, axis=-1).astype(BF16)

    def heads(dst_ref, src, col0, n):
        for i in range(n):
            dst_ref[i] = src[:, col0 + i * HEAD_DIM: col0 + (i + 1) * HEAD_DIM]

    tm = x.shape[0]
    eye = _eye(HEAD_DIM)

    def head_cols(src, col0, i):
        return src[:, col0 + i * HEAD_DIM: col0 + (i + 1) * HEAD_DIM]

    def transposed_tiles(dst_ref, h, cols):
        ts = [_dot_nt(eye, c).astype(BF16) for c in cols]
        for t in range(tm // SEL_TILE):
            dst_ref[h, t] = jnp.concatenate([x[:, t * SEL_TILE:(t + 1) * SEL_TILE] for x in ts], axis=1)

    for h in range(KV_HEADS):
        transposed_tiles(qa_ref, h, [head_cols(zn, 0, h * GROUP + g) for g in range(GROUP)])
    heads(qc_ref, zn, ATT_WIDTH, Q_HEADS)
    heads(ka_ref, zn, 2 * ATT_WIDTH, KV_HEADS)
    pos = pl.program_id(1) * tm + lax.broadcasted_iota(jnp.int32, (tm, HEAD_DIM), 0)
    blk = lax.broadcasted_iota(jnp.int32, (tm, HEAD_DIM), 1)
    onehot = jnp.where((pos >> (SEL_BLOCK.bit_length() - 1)) == blk, 1.0, 0.0).astype(BF16)
    c = 2 * ATT_WIDTH + LANES
    for i in range(KV_HEADS):
        ksel_ref[i] = jnp.concatenate([zn[:, c + i * HEAD_DIM: c + (i + 1) * HEAD_DIM], onehot], axis=-1)
    heads(kwin_ref, zn, 2 * ATT_WIDTH + 2 * LANES, KV_HEADS)
    c = NORM_WIDTH
    vb = proj[:, c:c + 3 * LANES].astype(BF16)
    for h in range(KV_HEADS):
        for dst_ref, col0 in ((va_ref, 0), (vsel_ref, LANES), (vwin_ref, 2 * LANES)):
            dst_ref[h] = _dot_nt(eye, head_cols(vb, col0, h)).astype(BF16)
    c += 3 * LANES
    heads(kvc_ref, proj, c, 2 * KV_HEADS)
    c += 2 * LANES
    su_ref[...] = proj[:, c:c + S5_CHANNELS]
    c += S5_CHANNELS
    gate_ref[...] = _transpose_exact(jax.nn.sigmoid(proj[:, c:c + LANES]))


def _inproj_weights(w_in, a_q_gain, a_k_gain, c_q_gain, c_k_gain):
    D = w_in.shape[0]
    o = np.cumsum([0, ATT_WIDTH, LANES, LANES, S5_CHANNELS, ATT_WIDTH, 3 * LANES, 3 * LANES, Q_HEADS * N_BRANCH])
    aq, ak, av, su, cq, ck, cv, cg = [w_in[:, o[i]:o[i + 1]] for i in range(8)]
    ck = ck.reshape(D, N_BRANCH, LANES)
    cv = cv.reshape(D, N_BRANCH, LANES)
    gpad = jnp.zeros((D, LANES - cg.shape[1]), w_in.dtype)
    w = jnp.concatenate([aq, cq, ak, ck[:, 1], ck[:, 2],
                         av, cv[:, 1], cv[:, 2], ck[:, 0], cv[:, 0], su, cg, gpad], axis=1)
    scale = HEAD_DIM ** -0.5
    gains = jnp.concatenate([jnp.tile(a_q_gain, Q_HEADS) * scale, jnp.tile(c_q_gain, Q_HEADS) * scale,
                             jnp.tile(a_k_gain, KV_HEADS), jnp.tile(c_k_gain[1], KV_HEADS),
                             jnp.tile(c_k_gain[2], KV_HEADS)]).reshape(1, NORM_WIDTH)
    return w.astype(BF16), gains.astype(F32)


def _head_mean_matrix():
    n = 2 * LANES
    m = (np.arange(n)[:, None] // HEAD_DIM == np.arange(n)[None, :] // HEAD_DIM).astype(np.float32) / HEAD_DIM
    return jnp.asarray(m, BF16)


def _inproj_call(x, gain, sc, sh, w, qk_gain, tm):
    B, S, D = x.shape
    hs = lambda n, dt: jax.ShapeDtypeStruct((B, n, S, HEAD_DIM), dt)
    hspec = lambda n: pl.BlockSpec((None, n, tm, HEAD_DIM), lambda b, i: (b, 0, i, 0))
    ts = lambda w: jax.ShapeDtypeStruct((B, KV_HEADS, S // SEL_TILE, HEAD_DIM, w), BF16)
    tspec = lambda w: pl.BlockSpec((None, KV_HEADS, tm // SEL_TILE, HEAD_DIM, w), lambda b, i: (b, 0, i, 0, 0))
    vs = jax.ShapeDtypeStruct((B, KV_HEADS, HEAD_DIM, S), BF16)
    vspec = pl.BlockSpec((None, KV_HEADS, HEAD_DIM, tm), lambda b, i: (b, 0, 0, i))
    vec = pl.BlockSpec((None, 1, D), lambda b, i: (b, 0, 0))
    return pl.pallas_call(
        _inproj_kernel,
        grid=(B, S // tm),
        in_specs=[pl.BlockSpec((None, tm, D), lambda b, i: (b, i, 0)),
                  pl.BlockSpec((1, D), lambda b, i: (0, 0)), vec, vec,
                  pl.BlockSpec((D, IN_COLS), lambda b, i: (0, 0)),
                  pl.BlockSpec((2 * LANES, 2 * LANES), lambda b, i: (0, 0)),
                  pl.BlockSpec((1, NORM_WIDTH), lambda b, i: (0, 0))],
        out_specs=[tspec(GROUP * SEL_TILE), hspec(Q_HEADS), hspec(KV_HEADS),
                   pl.BlockSpec((None, KV_HEADS, tm, LANES), lambda b, i: (b, 0, i, 0)),
                   hspec(KV_HEADS), vspec, vspec, vspec, hspec(2 * KV_HEADS),
                   pl.BlockSpec((tm, S5_CHANNELS), lambda b, i: (i, b)),
                   pl.BlockSpec((None, LANES, tm), lambda b, i: (b, 0, i))],
        out_shape=[ts(GROUP * SEL_TILE), hs(Q_HEADS, BF16), hs(KV_HEADS, BF16),
                   jax.ShapeDtypeStruct((B, KV_HEADS, S, LANES), BF16),
                   hs(KV_HEADS, BF16), vs, vs, vs, hs(2 * KV_HEADS, F32),
                   jax.ShapeDtypeStruct((S, B * S5_CHANNELS), F32),
                   jax.ShapeDtypeStruct((B, LANES, S), F32)],
        compiler_params=_params("arbitrary", "arbitrary"),
        name="inproj",
    )(x, gain.reshape(1, D), sc, sh, w, _head_mean_matrix(), qk_gain)


def _band_kernel(*refs, tq, span, window, has_sink, gate_branch, has_norm):
    refs = list(refs)
    sink_ref = refs.pop(0) if has_sink else None
    q_ref, k_ref, v_ref = refs[:3]
    refs = refs[3:]
    gate_ref = refs.pop(0) if gate_branch is not None else None
    gain_ref = refs.pop(0) if has_norm else None
    o_ref = refs[0]

    q0 = pl.program_id(1) * tq
    start = pl.multiple_of(jnp.maximum(q0 + tq - span, 0), LANES)
    diff = (q0 - start) + (lax.broadcasted_iota(jnp.int32, (span, tq), 1)
                           - lax.broadcasted_iota(jnp.int32, (span, tq), 0))
    bias = jnp.where(jnp.where(diff >= 0, diff, window) < window, 0.0, NEG_INF)

    chains = [(h, g) for h in range(KV_HEADS) for g in range(GROUP)]
    scores = [_dot(k_ref[h, pl.ds(start, span), :], q_ref[h, :, g * tq:(g + 1) * tq]) for h, g in chains]
    probs, dens = [], []
    for (h, g), s in zip(chains, scores):
        s = s + bias
        m = jnp.max(s, axis=0, keepdims=True)
        if has_sink:
            sink = sink_ref[h * GROUP + g]
            m = jnp.maximum(m, sink)
        e = jnp.exp(s - m)
        den = jnp.sum(e, axis=0, keepdims=True)
        if has_sink:
            den = den + jnp.exp(sink - m)
        probs.append(e.astype(BF16))
        dens.append(den)
    outs = []
    for (h, g), e, den in zip(chains, probs, dens):
        scale = 1.0 / den
        if gate_branch is not None:
            col = (h * GROUP + g) * N_BRANCH + gate_branch
            scale = scale * gate_ref[col:col + 1, :]
        outs.append(_dot(v_ref[h, :, pl.ds(start, span)], e) * scale)
    o_t = jnp.concatenate(outs, axis=0)
    if has_norm:
        ms = jnp.mean(o_t * o_t, axis=0, keepdims=True)
        o_t = o_t * lax.rsqrt(ms + EPS) * gain_ref[...]
    o_ref[...] = o_t.T.astype(o_ref.dtype)


def _band_call(q, k, v, *, window, sinks=None, gates=None, gate_branch=None, norm_gain=None, out_dtype=F32):
    B, _, n_q, _, width = q.shape
    tq = width // GROUP
    S = n_q * tq
    span = min(S, tq + -(-(window - 1) // LANES) * LANES)
    kern = functools.partial(_band_kernel, tq=tq, span=span, window=window, has_sink=sinks is not None,
                             gate_branch=gate_branch, has_norm=norm_gain is not None)
    in_specs, args = [], []
    if sinks is not None:
        in_specs.append(pl.BlockSpec(memory_space=pltpu.SMEM))
        args.append(sinks.astype(F32))
    in_specs += [pl.BlockSpec((None, KV_HEADS, None, HEAD_DIM, width), lambda b, i: (b, 0, i, 0, 0)),
                 pl.BlockSpec((None, KV_HEADS, S, HEAD_DIM), lambda b, i: (b, 0, 0, 0)),
                 pl.BlockSpec((None, KV_HEADS, HEAD_DIM, S), lambda b, i: (b, 0, 0, 0))]
    args += [q, k, v]
    if gates is not None:
        in_specs.append(pl.BlockSpec((None, LANES, tq), lambda b, i: (b, 0, i)))
        args.append(gates)
    if norm_gain is not None:
        in_specs.append(pl.BlockSpec((ATT_WIDTH, tq), lambda b, i: (0, 0)))
        args.append(jnp.broadcast_to(norm_gain.astype(F32)[:, None], (ATT_WIDTH, tq)))
    return pl.pallas_call(
        kern,
        grid=(B, S // tq),
        in_specs=in_specs,
        out_specs=pl.BlockSpec((None, tq, ATT_WIDTH), lambda b, i: (b, i, 0)),
        out_shape=jax.ShapeDtypeStruct((B, S, ATT_WIDTH), out_dtype),
        compiler_params=_params("arbitrary", "arbitrary"),
        name=f"band_attn_w{window}",
    )(*args)


def _s5_kernel(u_ref, bm_ref, are_ref, aim_ref, cm_ref, d_ref, wg_ref, bg_ref, gain_ref, o_ref,
               h_ref, st_ref, *, tc, nb):
    @pl.when(pl.program_id(0) == 0)
    def _():
        st_ref[...] = jnp.zeros_like(st_ref)

    u = pltpu.einshape("t(bc)->(tb)c", u_ref[...], b=nb)
    h_ref[...] = _dot(u.astype(BF16), bm_ref[...])
    a_re = are_ref[...]
    a_im = aim_ref[...]

    def step(t, carry):
        h_re, h_im = carry
        r = pl.ds(pl.multiple_of(t * nb, nb), nb)
        b_re = h_ref[r, 0:S5_WIDTH]
        b_im = h_ref[r, S5_WIDTH:2 * S5_WIDTH]
        n_re = a_re * h_re - a_im * h_im + b_re
        n_im = a_re * h_im + a_im * h_re + b_im
        h_ref[r, 0:S5_WIDTH] = n_re
        h_ref[r, S5_WIDTH:2 * S5_WIDTH] = n_im
        return n_re, n_im

    h_re, h_im = lax.fori_loop(0, tc, step, (st_ref[:, 0:S5_WIDTH], st_ref[:, S5_WIDTH:2 * S5_WIDTH]))
    st_ref[:, 0:S5_WIDTH] = h_re
    st_ref[:, S5_WIDTH:2 * S5_WIDTH] = h_im

    y = _dot(h_ref[...].astype(BF16), cm_ref[...]) + d_ref[...] * u
    hg = jax.nn.gelu(y)
    out = hg * jax.nn.sigmoid(_dot(hg.astype(BF16), wg_ref[...]) + bg_ref[...])
    ms = jnp.mean(out * out, axis=-1, keepdims=True)
    out = (out * lax.rsqrt(ms + EPS) * gain_ref[...]).astype(o_ref.dtype)
    o_ref[...] = pltpu.einshape("(tb)c->t(bc)", out, b=nb)


def _s5_weights(a_re, a_im, log_step, b_re, b_im, c_re, c_im):
    lam_re = jnp.minimum(a_re, S5_MIN_NEG)
    lam_im = a_im
    step = jnp.exp(log_step)[:, None]
    mag = jnp.exp(lam_re * step)
    abar_re = mag * jnp.cos(lam_im * step)
    abar_im = mag * jnp.sin(lam_im * step)
    den = lam_re * lam_re + lam_im * lam_im
    nr = abar_re - 1.0
    coef_re = (nr * lam_re + abar_im * lam_im) / den
    coef_im = (abar_im * lam_re - nr * lam_im) / den
    bb_re = coef_re[..., None] * b_re - coef_im[..., None] * b_im
    bb_im = coef_re[..., None] * b_im + coef_im[..., None] * b_re
    eye = jnp.eye(S5_GROUPS, dtype=F32)
    bd = lambda m: jnp.einsum('gnp,gh->gphn', m, eye).reshape(S5_CHANNELS, S5_WIDTH)
    bmat = jnp.concatenate([bd(bb_re), bd(bb_im)], axis=1)
    cd = lambda m: jnp.einsum('gpn,gh->gnhp', m, eye).reshape(S5_WIDTH, S5_CHANNELS)
    cmat = jnp.concatenate([cd(c_re), -cd(c_im)], axis=0)
    return bmat.astype(BF16), abar_re.reshape(1, S5_WIDTH), abar_im.reshape(1, S5_WIDTH), cmat.astype(BF16)


def _s5_call(u_tb, nb, bmat, abar_re, abar_im, cmat, d_skip, w_glu, b_glu, gain, tc):
    S = u_tb.shape[0]
    C = S5_CHANNELS
    const = lambda shape: pl.BlockSpec(shape, lambda i: (0,) * len(shape))
    return pl.pallas_call(
        functools.partial(_s5_kernel, tc=tc, nb=nb),
        grid=(S // tc,),
        in_specs=[pl.BlockSpec((tc, nb * C), lambda i: (i, 0)),
                  const((C, 2 * S5_WIDTH)), const((nb, S5_WIDTH)), const((nb, S5_WIDTH)),
                  const((2 * S5_WIDTH, C)), const((1, C)), const((C, C)), const((1, C)), const((1, C))],
        out_specs=pl.BlockSpec((tc, nb * C), lambda i: (i, 0)),
        out_shape=jax.ShapeDtypeStruct((S, nb * C), BF16),
        scratch_shapes=[pltpu.VMEM((tc * nb, 2 * S5_WIDTH), F32), pltpu.VMEM((nb, 2 * S5_WIDTH), F32)],
        compiler_params=_params("arbitrary"),
        name="s5_mixer",
    )(u_tb, bmat, jnp.broadcast_to(abar_re, (nb, S5_WIDTH)), jnp.broadcast_to(abar_im, (nb, S5_WIDTH)),
      cmat, d_skip.reshape(1, C), w_glu.astype(BF16), b_glu.reshape(1, C), gain.reshape(1, C))


def _compress_kernel(x_ref, pe_ref, w1_ref, b1_ref, w2_ref, b2_ref, kg_ref, k_ref, v_ref):
    half = CMP_STRIDE * HEAD_DIM
    n = x_ref.shape[1]
    for seg in range(2 * KV_HEADS):
        kind = seg // KV_HEADS
        x = x_ref[seg]
        z0 = _dot((x + pe_ref[kind, 0:1]).astype(BF16), w1_ref[kind, 0:half])
        z1 = _dot((x + pe_ref[kind, 1:2]).astype(BF16), w1_ref[kind, half:2 * half])
        z = z0 + pltpu.roll(z1, n - 1, 0)
        act = jax.nn.gelu(z + b1_ref[kind])
        out = _dot(act.astype(BF16), w2_ref[kind]) + b2_ref[kind]
        if kind == 0:
            ms = jnp.mean(out * out, axis=-1, keepdims=True)
            k_ref[seg] = (out * lax.rsqrt(ms + EPS) * kg_ref[...]).astype(BF16)
        else:
            v_ref[seg - KV_HEADS] = _dot_nt(_eye(HEAD_DIM), out.astype(BF16)).astype(BF16)


def _compress_call(kvc, pe, w1, b1, w2, b2, k_gain0):
    B, _, S, _ = kvc.shape
    n = S // CMP_STRIDE
    half = CMP_STRIDE * HEAD_DIM
    x = kvc.reshape(B, 2 * KV_HEADS, n, half)
    const = lambda shape: pl.BlockSpec(shape, lambda b: (0,) * len(shape))
    out = jax.ShapeDtypeStruct((B, KV_HEADS, n, HEAD_DIM), BF16)
    return pl.pallas_call(
        _compress_kernel,
        grid=(B,),
        in_specs=[pl.BlockSpec((None, 2 * KV_HEADS, n, half), lambda b: (b, 0, 0, 0)),
                  const((2, 2, half)), const((2, 2 * half, CMP_HIDDEN)), const((2, 1, CMP_HIDDEN)),
                  const((2, CMP_HIDDEN, HEAD_DIM)), const((2, 1, HEAD_DIM)), const((1, HEAD_DIM))],
        out_specs=[pl.BlockSpec((None, KV_HEADS, n, HEAD_DIM), lambda b: (b, 0, 0, 0)),
                   pl.BlockSpec((None, KV_HEADS, HEAD_DIM, n), lambda b: (b, 0, 0, 0))],
        out_shape=[out, jax.ShapeDtypeStruct((B, KV_HEADS, HEAD_DIM, n), BF16)],
        compiler_params=_params("arbitrary"),
        name="nsa_compress",
    )(x, pe.reshape(2, 2, half), w1.astype(BF16), b1.reshape(2, 1, CMP_HIDDEN), w2.astype(BF16),
      b2.reshape(2, 1, HEAD_DIM), k_gain0.reshape(1, HEAD_DIM))


def _cmp_select_kernel(q_ref, k_ref, vt_ref, cov_ref, gate_ref, o_ref, qa_ref, *, tq, topk):
    n_cmp = k_ref.shape[1]
    n_sel = cov_ref.shape[0]
    q0 = pl.program_id(1) * tq
    cidx = lax.broadcasted_iota(jnp.int32, (n_cmp, GROUP * tq), 0)
    t_col = q0 + (lax.broadcasted_iota(jnp.int32, (n_cmp, GROUP * tq), 1) & (tq - 1))
    ok = cidx * CMP_STRIDE + (CMP_BLOCK - 1) <= t_col
    t_row = q0 + (lax.broadcasted_iota(jnp.int32, (1, GROUP * tq), 1) & (tq - 1))
    any_ok = jnp.where(t_row >= CMP_BLOCK - 1, 1.0, 0.0)

    j = lax.broadcasted_iota(jnp.int32, (n_sel, tq), 0)
    qblk = (q0 + lax.broadcasted_iota(jnp.int32, (n_sel, tq), 1)) >> (SEL_BLOCK.bit_length() - 1)
    forced = (j == 0) | (j == qblk) | (j == qblk - 1)
    eye = _eye(HEAD_DIM)

    outs = []
    for h in range(KV_HEADS):
        q = q_ref[h * GROUP:(h + 1) * GROUP].reshape(GROUP * tq, HEAD_DIM)
        qt = _dot_nt(eye, q).astype(BF16)
        s = jnp.where(ok, _dot(k_ref[h], qt), NEG_INF)
        m = jnp.max(s, axis=0, keepdims=True)
        e = jnp.exp(s - m)
        p = e * (any_ok / jnp.sum(e, axis=0, keepdims=True))
        ot = _dot(vt_ref[h], p.astype(BF16))
        for g in range(GROUP):
            col = (h * GROUP + g) * N_BRANCH
            outs.append(ot[:, g * tq:(g + 1) * tq] * gate_ref[col:col + 1, :])

        psum = p[:, 0:tq] + p[:, tq:2 * tq] + p[:, 2 * tq:3 * tq]
        p_hi, p_lo = _split_bf16(psum)
        imp = _dot(cov_ref[...], p_hi) + _dot(cov_ref[...], p_lo)
        imp = jnp.where(forced, imp + FORCE_BONUS, imp)
        imp = jnp.where(j <= qblk, imp, NEG_INF)
        sub = 8
        groups = [imp[r0:r0 + sub] for r0 in range(0, n_sel, sub)]
        ranks = [jnp.zeros((sub, tq), F32) for _ in groups]
        jloc = lax.broadcasted_iota(jnp.int32, (sub, tq), 0)
        for i in range(n_sel):
            r = imp[i:i + 1, :]
            for b, grp in enumerate(groups):
                if b > i // sub:
                    beats = jnp.where(r >= grp, 1.0, 0.0)
                elif b < i // sub:
                    beats = jnp.where(r > grp, 1.0, 0.0)
                else:
                    beats = jnp.where(jloc > i % sub, jnp.where(r >= grp, 1.0, 0.0), jnp.where(r > grp, 1.0, 0.0))
                ranks[b] = ranks[b] + beats
        rank = jnp.concatenate(ranks, axis=0)
        bias = jnp.where(rank < topk, 0.0, UNSELECTED_BIAS).astype(BF16)
        if n_sel < HEAD_DIM:
            bias = jnp.concatenate([bias, jnp.zeros((HEAD_DIM - n_sel, tq), BF16)], axis=0)
        qa_ref[h] = jnp.concatenate([qt, jnp.concatenate([bias] * GROUP, axis=1)], axis=0)
    o_ref[...] = jnp.concatenate(outs, axis=0).T


def _cover_matrix_t(n_cmp, n_sel):
    cs = np.arange(n_cmp)[None, :] * CMP_STRIDE
    ss = np.arange(n_sel)[:, None] * SEL_BLOCK
    cover = np.clip(np.minimum(cs + CMP_BLOCK, ss + SEL_BLOCK) - np.maximum(cs, ss), 0, None)
    return jnp.asarray(cover / CMP_BLOCK, BF16)


def _cmp_select_call(q, k_cmp, v_cmp_t, gates):
    B, _, S, _ = q.shape
    tq = SEL_TILE
    n_cmp = k_cmp.shape[2]
    n_sel = S // SEL_BLOCK
    assert n_sel <= HEAD_DIM
    topk = min(SEL_TOPK, n_sel)
    return pl.pallas_call(
        functools.partial(_cmp_select_kernel, tq=tq, topk=topk),
        grid=(B, S // tq),
        in_specs=[pl.BlockSpec((None, Q_HEADS, tq, HEAD_DIM), lambda b, i: (b, 0, i, 0)),
                  pl.BlockSpec((None, KV_HEADS, n_cmp, HEAD_DIM), lambda b, i: (b, 0, 0, 0)),
                  pl.BlockSpec((None, KV_HEADS, HEAD_DIM, n_cmp), lambda b, i: (b, 0, 0, 0)),
                  pl.BlockSpec((n_sel, n_cmp), lambda b, i: (0, 0)),
                  pl.BlockSpec((None, LANES, tq), lambda b, i: (b, 0, i))],
        out_specs=[pl.BlockSpec((None, tq, ATT_WIDTH), lambda b, i: (b, i, 0)),
                   pl.BlockSpec((None, KV_HEADS, None, 2 * HEAD_DIM, GROUP * tq), lambda b, i: (b, 0, i, 0, 0))],
        out_shape=[jax.ShapeDtypeStruct((B, S, ATT_WIDTH), F32),
                   jax.ShapeDtypeStruct((B, KV_HEADS, S // tq, 2 * HEAD_DIM, GROUP * tq), BF16)],
        compiler_params=_params("arbitrary", "arbitrary"),
        name="nsa_cmp_select",
    )(q, k_cmp, v_cmp_t, _cover_matrix_t(n_cmp, n_sel), gates)


def _selected_kernel(qa_ref, k_ref, vt_ref, gate_ref, o_ref, m_ref, l_ref, acc_ref, *, tq):
    i = pl.program_id(1)
    width = GROUP * tq
    krow = lax.broadcasted_iota(jnp.int32, (tq, width), 0)
    qcol = lax.broadcasted_iota(jnp.int32, (tq, width), 1) & (tq - 1)
    causal = krow <= qcol

    m_ref[...] = jnp.full_like(m_ref, NEG_INF)
    l_ref[...] = jnp.zeros_like(l_ref)
    acc_ref[...] = jnp.zeros_like(acc_ref)

    def update(k0, nk, diagonal):
        k0 = pl.multiple_of(k0, tq)
        s_all = [_dot(k_ref[h, pl.ds(k0, nk), :], qa_ref[h]) for h in range(KV_HEADS)]
        p_all = []
        for h in range(KV_HEADS):
            s = jnp.where(causal, s_all[h], NEG_INF) if diagonal else s_all[h]
            m_prev = m_ref[h]
            m_new = jnp.maximum(m_prev, jnp.max(s, axis=0, keepdims=True))
            alpha = jnp.exp(m_prev - m_new)
            p = jnp.exp(s - m_new)
            l_ref[h] = alpha * l_ref[h] + jnp.sum(p, axis=0, keepdims=True)
            m_ref[h] = m_new
            p_all.append((alpha, p.astype(BF16)))
        for h in range(KV_HEADS):
            alpha, p = p_all[h]
            acc_ref[h] = alpha * acc_ref[h] + _dot(vt_ref[h, :, pl.ds(k0, nk)], p)

    def body(j, carry):
        update(j * (2 * tq), 2 * tq, False)
        return carry

    lax.fori_loop(0, i >> 1, body, 0)

    @pl.when((i & 1) == 1)
    def _():
        update((i - 1) * tq, tq, False)

    update(i * tq, tq, True)

    outs = []
    for h in range(KV_HEADS):
        ot = acc_ref[h] / l_ref[h]
        for g in range(GROUP):
            col = (h * GROUP + g) * N_BRANCH + 1
            outs.append(ot[:, g * tq:(g + 1) * tq] * gate_ref[col:col + 1, :])
    o_ref[...] = jnp.concatenate(outs, axis=0).T


def _selected_call(q_aug, k_aug, v_t, gates):
    B, _, n_tiles, _, width = q_aug.shape
    tq = width // GROUP
    S = n_tiles * tq
    return pl.pallas_call(
        functools.partial(_selected_kernel, tq=tq),
        grid=(B, n_tiles),
        in_specs=[pl.BlockSpec((None, KV_HEADS, None, 2 * HEAD_DIM, width), lambda b, i: (b, 0, i, 0, 0)),
                  pl.BlockSpec((None, KV_HEADS, S, 2 * HEAD_DIM), lambda b, i: (b, 0, 0, 0)),
                  pl.BlockSpec((None, KV_HEADS, HEAD_DIM, S), lambda b, i: (b, 0, 0, 0)),
                  pl.BlockSpec((None, LANES, tq), lambda b, i: (b, 0, i))],
        out_specs=pl.BlockSpec((None, tq, ATT_WIDTH), lambda b, i: (b, i, 0)),
        out_shape=jax.ShapeDtypeStruct((B, S, ATT_WIDTH), F32),
        scratch_shapes=[pltpu.VMEM((KV_HEADS, 1, width), F32), pltpu.VMEM((KV_HEADS, 1, width), F32),
                        pltpu.VMEM((KV_HEADS, HEAD_DIM, width), F32)],
        compiler_params=_params("arbitrary", "arbitrary"),
        name="nsa_selected",
    )(q_aug, k_aug, v_t, gates)


def _mix_ffn_kernel(oa_ref, ob_ref, c1_ref, c2_ref, c3_ref, gc_ref, wo_ref, x_ref, ga1_ref,
                    g2_ref, sc_ref, sh_ref, ga2_ref, w1_ref, w2_ref, o_ref, *, tf):
    oc = c1_ref[...] + c2_ref[...] + c3_ref[...]
    ms = jnp.mean(oc * oc, axis=-1, keepdims=True)
    oc = (oc * lax.rsqrt(ms + EPS) * gc_ref[...]).astype(BF16)
    a_end = ATT_WIDTH
    b_end = ATT_WIDTH + S5_CHANNELS
    y = (_dot(oa_ref[...], wo_ref[0:a_end]) + _dot(ob_ref[...], wo_ref[a_end:b_end])
         + _dot(oc, wo_ref[b_end:]))
    x = x_ref[...] + ga1_ref[...] * y

    ms = jnp.mean(x * x, axis=-1, keepdims=True)
    h = x * lax.rsqrt(ms + EPS) * g2_ref[...]
    h = (h * (1.0 + sc_ref[...]) + sh_ref[...]).astype(BF16)
    acc = None
    for f0 in range(0, w1_ref.shape[1], tf):
        a = jnp.maximum(_dot(h, w1_ref[:, f0:f0 + tf]), 0.0)
        y = _dot((a * a).astype(BF16), w2_ref[f0:f0 + tf, :])
        acc = y if acc is None else acc + y
    o_ref[...] = x + ga2_ref[...] * acc


def _mix_ffn_call(o_a, o_b_tb, o_cmp, o_slc, o_win, gain_c, w_out, x, ga1, gain2, sc2, sh2, ga2, w1, w2, tm, tf):
    B, S, D = x.shape
    F = w1.shape[1]
    att = pl.BlockSpec((None, tm, ATT_WIDTH), lambda b, i: (b, i, 0))
    vec = pl.BlockSpec((None, 1, D), lambda b, i: (b, 0, 0))
    resident = dict(pipeline_mode=pl.Buffered(1))
    return pl.pallas_call(
        functools.partial(_mix_ffn_kernel, tf=tf),
        grid=(B, S // tm),
        in_specs=[att, pl.BlockSpec((tm, S5_CHANNELS), lambda b, i: (i, b)), att, att, att,
                  pl.BlockSpec((1, ATT_WIDTH), lambda b, i: (0, 0)),
                  pl.BlockSpec(w_out.shape, lambda b, i: (0, 0), **resident),
                  pl.BlockSpec((None, tm, D), lambda b, i: (b, i, 0)), vec,
                  pl.BlockSpec((1, D), lambda b, i: (0, 0)), vec, vec, vec,
                  pl.BlockSpec((D, F), lambda b, i: (0, 0), **resident),
                  pl.BlockSpec((F, D), lambda b, i: (0, 0), **resident)],
        out_specs=pl.BlockSpec((None, tm, D), lambda b, i: (b, i, 0)),
        out_shape=jax.ShapeDtypeStruct((B, S, D), F32),
        compiler_params=_params("arbitrary", "arbitrary"),
        name="mix_ffn",
    )(o_a, o_b_tb, o_cmp, o_slc, o_win, gain_c.reshape(1, ATT_WIDTH), w_out, x, ga1,
      gain2.reshape(1, D), sc2, sh2, ga2, w1, w2)


def _layer(x, mod, p, tiles):
    B, S, D = x.shape
    sh1, sc1, ga1, sh2, sc2, ga2 = [m.reshape(B, 1, D) for m in jnp.split(mod, 6, axis=-1)]

    w_in, qk_gain = _inproj_weights(p["w_in"], p["a_q_gain"], p["a_k_gain"], p["c_q_gain"], p["c_k_gain"])
    (qa, qc, ka, ksel, kwin, va, vsel, vwin, kvc, su, gates) = _inproj_call(
        x, p["norm1_g"], sc1, sh1, w_in, qk_gain, tiles["tm_in"])

    a_end = ATT_WIDTH
    b_end = ATT_WIDTH + S5_CHANNELS
    o_a = _band_call(qa, ka, va, window=A_WINDOW, sinks=p["a_sinks"], norm_gain=p["out_norm_g"][:a_end],
                     out_dtype=BF16)

    bmat, abar_re, abar_im, cmat = _s5_weights(p["s5_a_re"], p["s5_a_im"], p["s5_log_step"],
                                               p["s5_b_re"], p["s5_b_im"], p["s5_c_re"], p["s5_c_im"])
    o_b = _s5_call(su, B, bmat, abar_re, abar_im, cmat, p["s5_d"],
                   p["s5_w_glu"], p["s5_b_glu"], p["out_norm_g"][a_end:b_end], tiles["tc_s5"])

    k_cmp, v_cmp = _compress_call(kvc, p["cmp_pe"], p["cmp_w1"], p["cmp_b1"], p["cmp_w2"], p["cmp_b2"],
                                  p["c_k_gain"][0])
    o_cmp, q_aug = _cmp_select_call(qc, k_cmp, v_cmp, gates)
    o_slc = _selected_call(q_aug, ksel, vsel, gates)
    o_win = _band_call(q_aug, kwin, vwin, window=C_WINDOW, gates=gates, gate_branch=2)

    return _mix_ffn_call(o_a, o_b, o_cmp, o_slc, o_win, p["out_norm_g"][b_end:], p["w_out"].astype(BF16), x, ga1,
                         p["norm2_g"], sc2, sh2, ga2, p["w_ff1"].astype(BF16), p["w_ff2"].astype(BF16),
                         tiles["tm_ffn"], tiles["tf_ffn"])


def _tiles(S):
    return dict(tm_in=min(512, S), tc_s5=min(128, S), tm_ffn=min(512, S), tf_ffn=1024)


def kernel(x, c, norm1_g, norm2_g, w_ada, b_ada, w_in, a_q_gain, a_k_gain, a_sinks, s5_a_re, s5_a_im, s5_log_step, s5_b_re, s5_b_im, s5_c_re, s5_c_im, s5_d, s5_w_glu, s5_b_glu, c_q_gain, c_k_gain, cmp_pe, cmp_w1, cmp_b1, cmp_w2, cmp_b2, out_norm_g, w_out, w_ff1, w_ff2):
    per_layer = dict(norm1_g=norm1_g, norm2_g=norm2_g, w_in=w_in, a_q_gain=a_q_gain, a_k_gain=a_k_gain,
                     a_sinks=a_sinks, s5_a_re=s5_a_re, s5_a_im=s5_a_im, s5_log_step=s5_log_step,
                     s5_b_re=s5_b_re, s5_b_im=s5_b_im, s5_c_re=s5_c_re, s5_c_im=s5_c_im, s5_d=s5_d,
                     s5_w_glu=s5_w_glu, s5_b_glu=s5_b_glu, c_q_gain=c_q_gain, c_k_gain=c_k_gain,
                     cmp_pe=cmp_pe, cmp_w1=cmp_w1, cmp_b1=cmp_b1, cmp_w2=cmp_w2, cmp_b2=cmp_b2,
                     out_norm_g=out_norm_g, w_out=w_out, w_ff1=w_ff1, w_ff2=w_ff2)
    depth = w_in.shape[0]
    mod = _ada_call(c, w_ada, b_ada)
    tiles = _tiles(x.shape[1])
    for l in range(depth):
        x = _layer(x, mod[l], {k: v[l] for k, v in per_layer.items()}, tiles)
    return x
```

```python
import functools
import math

import numpy as np
import jax
import jax.numpy as jnp
from jax import lax
from jax.experimental import pallas as pl
from jax.experimental.pallas import tpu as pltpu

F32 = jnp.float32
BF16 = jnp.bfloat16

HEAD_DIM = 64
KV_HEADS = 2
GROUP = 3
Q_HEADS = KV_HEADS * GROUP
ATT_WIDTH = Q_HEADS * HEAD_DIM
A_WINDOW = 128
C_WINDOW = 512
S5_GROUP = 16
S5_GROUPS = 16
S5_CHANNELS = S5_GROUP * S5_GROUPS
S5_STATE = 64
S5_WIDTH = S5_GROUPS * S5_STATE
S5_MIN_NEG = -1e-4
N_BRANCH = 3
CMP_BLOCK = 32
CMP_STRIDE = 16
CMP_HIDDEN = 256
SEL_BLOCK = 64
SEL_TOPK = 16
FORCE_BONUS = 1e4
NEG_INF = -1e30
UNSELECTED_BIAS = -32768.0
EPS = 1e-6
LOG2E = math.log2(math.e)
LANES = 128
V7X_VMEM_BYTES = 64 * 1024 * 1024
VMEM_LIMIT = V7X_VMEM_BYTES - 8 * 1024 * 1024

SEL_TILE = 256
OFF_AQ = 0
OFF_AK = OFF_AQ + ATT_WIDTH
OFF_AV = OFF_AK + KV_HEADS * HEAD_DIM
OFF_SU = OFF_AV + KV_HEADS * HEAD_DIM
OFF_CQ = OFF_SU + S5_CHANNELS
OFF_CK = OFF_CQ + ATT_WIDTH
OFF_CV = OFF_CK + N_BRANCH * KV_HEADS * HEAD_DIM
OFF_CG = OFF_CV + N_BRANCH * KV_HEADS * HEAD_DIM
IN_COLS = OFF_CG + LANES


def _params(*semantics):
    return pltpu.CompilerParams(dimension_semantics=semantics, vmem_limit_bytes=VMEM_LIMIT)


def _dot(a, b):
    return jnp.dot(a, b, preferred_element_type=F32)


def _dot_nt(a, b):
    return lax.dot_general(a, b, (((1,), (1,)), ((), ())), preferred_element_type=F32)


def _eye(n):
    r = lax.broadcasted_iota(jnp.int32, (n, n), 0)
    c = lax.broadcasted_iota(jnp.int32, (n, n), 1)
    return jnp.where(r == c, 1.0, 0.0).astype(BF16)


def _split_bf16(x):
    hi = x.astype(BF16)
    lo = (x - hi.astype(F32)).astype(BF16)
    return hi, lo


def _transpose_exact(x):
    eye = _eye(x.shape[1])
    hi = x.astype(BF16)
    mid, lo = _split_bf16(x - hi.astype(F32))
    return _dot_nt(eye, hi) + _dot_nt(eye, mid) + _dot_nt(eye, lo)


def _ada_kernel(c_ref, w_ref, b_ref, o_ref):
    c = c_ref[...]
    a = c * jax.nn.sigmoid(c)
    a_hi, a_lo = _split_bf16(a)
    w_hi, w_lo = _split_bf16(w_ref[...])
    acc = _dot(a_hi, w_hi) + _dot(a_lo, w_hi) + _dot(a_hi, w_lo)
    o_ref[...] = acc + b_ref[...]


def _ada_call(c, w_ada, b_ada):
    L, D, N = w_ada.shape
    B = c.shape[0]
    tn = 768
    return pl.pallas_call(
        _ada_kernel,
        grid=(L, N // tn),
        in_specs=[pl.BlockSpec((B, D), lambda l, j: (0, 0)),
                  pl.BlockSpec((None, D, tn), lambda l, j: (l, 0, j)),
                  pl.BlockSpec((None, 1, tn), lambda l, j: (l, 0, j))],
        out_specs=pl.BlockSpec((None, B, tn), lambda l, j: (l, 0, j)),
        out_shape=jax.ShapeDtypeStruct((L, B, N), F32),
        compiler_params=_params("arbitrary", "arbitrary"),
        name="adaln",
    )(c, w_ada, b_ada.reshape(L, 1, N))


def _inproj_kernel(x_ref, g_ref, sc_ref, sh_ref, w_ref, j_ref, qkg_ref,
                   qa_ref, qc_ref, ka_ref, ksel_ref, kwin_ref, va_ref, vsel_ref, vwin_ref,
                   kvc_ref, su_ref, gate_ref):
    x = x_ref[...]
    ms = jnp.mean(x * x, axis=-1, keepdims=True)
    h = x * lax.rsqrt(ms + EPS) * g_ref[...]
    h = h * (1.0 + sc_ref[...]) + sh_ref[...]
    proj = _dot(h.astype(BF16), w_ref[...])

    j2 = j_ref[...]

    def qk_norm(c0, width):
        out = []
        for a in range(c0, c0 + width, 2 * LANES):
            w = min(2 * LANES, c0 + width - a)
            z = proj[:, a:a + w]
            msq = _dot((z * z).astype(BF16), j2[:w, :w])
            out.append(z * lax.rsqrt(msq + EPS) * qkg_ref[:, a:a + w])
        return (out[0] if len(out) == 1 else jnp.concatenate(out, axis=-1)).astype(BF16)

    def head(src, i):
        return src[:, i * HEAD_DIM:(i + 1) * HEAD_DIM]

    tm = x.shape[0]
    eye = _eye(HEAD_DIM)

    q_a = qk_norm(OFF_AQ, ATT_WIDTH)
    for h in range(KV_HEADS):
        ts = [_dot_nt(eye, head(q_a, h * GROUP + g)).astype(BF16) for g in range(GROUP)]
        for t in range(tm // SEL_TILE):
            qa_ref[h, t] = jnp.concatenate([x_[:, t * SEL_TILE:(t + 1) * SEL_TILE] for x_ in ts], axis=1)
    q_c = qk_norm(OFF_CQ, ATT_WIDTH)
    for i in range(Q_HEADS):
        qc_ref[i] = head(q_c, i)
    k_a = qk_norm(OFF_AK, LANES)
    k_c = qk_norm(OFF_CK + LANES, 2 * LANES)
    pos = pl.program_id(1) * tm + lax.broadcasted_iota(jnp.int32, (tm, HEAD_DIM), 0)
    blk = lax.broadcasted_iota(jnp.int32, (tm, HEAD_DIM), 1)
    onehot = jnp.where((pos >> (SEL_BLOCK.bit_length() - 1)) == blk, 1.0, 0.0).astype(BF16)
    for h in range(KV_HEADS):
        ka_ref[h] = head(k_a, h)
        ksel_ref[h] = jnp.concatenate([head(k_c, h), onehot], axis=-1)
        kwin_ref[h] = head(k_c, KV_HEADS + h)
        kvc_ref[h] = head(proj[:, OFF_CK:OFF_CK + LANES], h)
        kvc_ref[KV_HEADS + h] = head(proj[:, OFF_CV:OFF_CV + LANES], h)
    v_a = proj[:, OFF_AV:OFF_AV + LANES].astype(BF16)
    v_c = proj[:, OFF_CV + LANES:OFF_CV + 3 * LANES].astype(BF16)
    for h in range(KV_HEADS):
        va_ref[h] = _dot_nt(eye, head(v_a, h)).astype(BF16)
        vsel_ref[h] = _dot_nt(eye, head(v_c, h)).astype(BF16)
        vwin_ref[h] = _dot_nt(eye, head(v_c, KV_HEADS + h)).astype(BF16)
    su_ref[...] = proj[:, OFF_SU:OFF_SU + S5_CHANNELS]
    gate_ref[...] = _transpose_exact(jax.nn.sigmoid(proj[:, OFF_CG:OFF_CG + LANES]))


def _inproj_weights(w_in, a_q_gain, a_k_gain, c_q_gain, c_k_gain):
    w = jnp.pad(w_in.astype(BF16), ((0, 0), (0, IN_COLS - w_in.shape[1])))
    scale = HEAD_DIM ** -0.5 * LOG2E
    gains = jnp.ones((IN_COLS,), F32)
    gains = gains.at[OFF_AQ:OFF_AQ + ATT_WIDTH].set(jnp.tile(a_q_gain, Q_HEADS) * scale)
    gains = gains.at[OFF_CQ:OFF_CQ + ATT_WIDTH].set(jnp.tile(c_q_gain, Q_HEADS) * scale)
    gains = gains.at[OFF_AK:OFF_AK + LANES].set(jnp.tile(a_k_gain, KV_HEADS))
    gains = gains.at[OFF_CK + LANES:OFF_CK + 3 * LANES].set(
        jnp.concatenate([jnp.tile(c_k_gain[1], KV_HEADS), jnp.tile(c_k_gain[2], KV_HEADS)]))
    return w, gains.reshape(1, IN_COLS)


def _head_mean_matrix():
    n = 2 * LANES
    m = (np.arange(n)[:, None] // HEAD_DIM == np.arange(n)[None, :] // HEAD_DIM).astype(np.float32) / HEAD_DIM
    return jnp.asarray(m, BF16)


def _inproj_call(x, gain, sc, sh, w, qk_gain, tm):
    B, S, D = x.shape
    hs = lambda n, dt: jax.ShapeDtypeStruct((B, n, S, HEAD_DIM), dt)
    hspec = lambda n: pl.BlockSpec((None, n, tm, HEAD_DIM), lambda b, i: (b, 0, i, 0))
    ts = lambda w: jax.ShapeDtypeStruct((B, KV_HEADS, S // SEL_TILE, HEAD_DIM, w), BF16)
    tspec = lambda w: pl.BlockSpec((None, KV_HEADS, tm // SEL_TILE, HEAD_DIM, w), lambda b, i: (b, 0, i, 0, 0))
    vs = jax.ShapeDtypeStruct((B, KV_HEADS, HEAD_DIM, S), BF16)
    vspec = pl.BlockSpec((None, KV_HEADS, HEAD_DIM, tm), lambda b, i: (b, 0, 0, i))
    vec = pl.BlockSpec((None, 1, D), lambda b, i: (b, 0, 0))
    return pl.pallas_call(
        _inproj_kernel,
        grid=(B, S // tm),
        in_specs=[pl.BlockSpec((None, tm, D), lambda b, i: (b, i, 0)),
                  pl.BlockSpec((1, D), lambda b, i: (0, 0)), vec, vec,
                  pl.BlockSpec((D, IN_COLS), lambda b, i: (0, 0)),
                  pl.BlockSpec((2 * LANES, 2 * LANES), lambda b, i: (0, 0)),
                  pl.BlockSpec((1, IN_COLS), lambda b, i: (0, 0))],
        out_specs=[tspec(GROUP * SEL_TILE), hspec(Q_HEADS), hspec(KV_HEADS),
                   pl.BlockSpec((None, KV_HEADS, tm, LANES), lambda b, i: (b, 0, i, 0)),
                   hspec(KV_HEADS), vspec, vspec, vspec, hspec(2 * KV_HEADS),
                   pl.BlockSpec((tm, S5_CHANNELS), lambda b, i: (i, b)),
                   pl.BlockSpec((None, LANES, tm), lambda b, i: (b, 0, i))],
        out_shape=[ts(GROUP * SEL_TILE), hs(Q_HEADS, BF16), hs(KV_HEADS, BF16),
                   jax.ShapeDtypeStruct((B, KV_HEADS, S, LANES), BF16),
                   hs(KV_HEADS, BF16), vs, vs, vs, hs(2 * KV_HEADS, F32),
                   jax.ShapeDtypeStruct((S, B * S5_CHANNELS), F32),
                   jax.ShapeDtypeStruct((B, LANES, S), F32)],
        compiler_params=_params("arbitrary", "arbitrary"),
        name="inproj",
    )(x, gain.reshape(1, D), sc, sh, w, _head_mean_matrix(), qk_gain)


def _band_kernel(*refs, tq, span, window, has_sink, gate_branch, has_norm):
    refs = list(refs)
    sink_ref = refs.pop(0) if has_sink else None
    q_ref, k_ref, v_ref = refs[:3]
    refs = refs[3:]
    gate_ref = refs.pop(0) if gate_branch is not None else None
    gain_ref = refs.pop(0) if has_norm else None
    o_ref = refs[0]

    q0 = pl.program_id(1) * tq
    start = pl.multiple_of(jnp.maximum(q0 + tq - span, 0), LANES)
    diff = (q0 - start) + (lax.broadcasted_iota(jnp.int32, (span, tq), 1)
                           - lax.broadcasted_iota(jnp.int32, (span, tq), 0))
    bias = jnp.where(jnp.where(diff >= 0, diff, window) < window, 0.0, NEG_INF)

    chains = [(h, g) for h in range(KV_HEADS) for g in range(GROUP)]
    scores = [_dot(k_ref[h, pl.ds(start, span), :], q_ref[h, :, g * tq:(g + 1) * tq]) for h, g in chains]
    probs, dens = [], []
    for (h, g), s in zip(chains, scores):
        s = s + bias
        m = jnp.max(s, axis=0, keepdims=True)
        if has_sink:
            sink = sink_ref[h * GROUP + g] * LOG2E
            m = jnp.maximum(m, sink)
        e = jnp.exp2(s - m)
        den = jnp.sum(e, axis=0, keepdims=True)
        if has_sink:
            den = den + jnp.exp2(sink - m)
        probs.append(e.astype(BF16))
        dens.append(den)
    outs = []
    for (h, g), e, den in zip(chains, probs, dens):
        scale = 1.0 / den
        if gate_branch is not None:
            col = (h * GROUP + g) * N_BRANCH + gate_branch
            scale = scale * gate_ref[col:col + 1, :]
        outs.append(_dot(v_ref[h, :, pl.ds(start, span)], e) * scale)
    o_t = jnp.concatenate(outs, axis=0)
    if has_norm:
        ms = jnp.mean(o_t * o_t, axis=0, keepdims=True)
        o_t = o_t * lax.rsqrt(ms + EPS) * gain_ref[...]
    o_ref[...] = o_t.T.astype(o_ref.dtype)


def _band_call(q, k, v, *, window, sinks=None, gates=None, gate_branch=None, norm_gain=None, out_dtype=F32):
    B, _, n_q, _, width = q.shape
    tq = width // GROUP
    S = n_q * tq
    span = min(S, tq + -(-(window - 1) // LANES) * LANES)
    kern = functools.partial(_band_kernel, tq=tq, span=span, window=window, has_sink=sinks is not None,
                             gate_branch=gate_branch, has_norm=norm_gain is not None)
    in_specs, args = [], []
    if sinks is not None:
        in_specs.append(pl.BlockSpec(memory_space=pltpu.SMEM))
        args.append(sinks.astype(F32))
    in_specs += [pl.BlockSpec((None, KV_HEADS, None, HEAD_DIM, width), lambda b, i: (b, 0, i, 0, 0)),
                 pl.BlockSpec((None, KV_HEADS, S, HEAD_DIM), lambda b, i: (b, 0, 0, 0)),
                 pl.BlockSpec((None, KV_HEADS, HEAD_DIM, S), lambda b, i: (b, 0, 0, 0))]
    args += [q, k, v]
    if gates is not None:
        in_specs.append(pl.BlockSpec((None, LANES, tq), lambda b, i: (b, 0, i)))
        args.append(gates)
    if norm_gain is not None:
        in_specs.append(pl.BlockSpec((ATT_WIDTH, tq), lambda b, i: (0, 0)))
        args.append(jnp.broadcast_to(norm_gain.astype(F32)[:, None], (ATT_WIDTH, tq)))
    return pl.pallas_call(
        kern,
        grid=(B, S // tq),
        in_specs=in_specs,
        out_specs=pl.BlockSpec((None, tq, ATT_WIDTH), lambda b, i: (b, i, 0)),
        out_shape=jax.ShapeDtypeStruct((B, S, ATT_WIDTH), out_dtype),
        compiler_params=_params("arbitrary", "arbitrary"),
        name=f"band_attn_w{window}",
    )(*args)


def _s5_kernel(u_ref, bm_ref, are_ref, aim_ref, cm_ref, d_ref, wg_ref, bg_ref, gain_ref, o_ref,
               h_ref, st_ref, *, tc, nb):
    @pl.when(pl.program_id(0) == 0)
    def _():
        st_ref[...] = jnp.zeros_like(st_ref)

    u = u_ref[...].reshape(tc * nb, S5_CHANNELS)
    h_ref[...] = _dot(u.astype(BF16), bm_ref[...])
    a_re = are_ref[...]
    a_im = aim_ref[...]

    def step(t, carry):
        h_re, h_im = carry
        r = pl.ds(pl.multiple_of(t * nb, nb), nb)
        b_re = h_ref[r, 0:S5_WIDTH]
        b_im = h_ref[r, S5_WIDTH:2 * S5_WIDTH]
        n_re = a_re * h_re - a_im * h_im + b_re
        n_im = a_re * h_im + a_im * h_re + b_im
        h_ref[r, 0:S5_WIDTH] = n_re
        h_ref[r, S5_WIDTH:2 * S5_WIDTH] = n_im
        return n_re, n_im

    h_re, h_im = lax.fori_loop(0, tc, step, (st_ref[:, 0:S5_WIDTH], st_ref[:, S5_WIDTH:2 * S5_WIDTH]))
    st_ref[:, 0:S5_WIDTH] = h_re
    st_ref[:, S5_WIDTH:2 * S5_WIDTH] = h_im

    y = _dot(h_ref[...].astype(BF16), cm_ref[...]) + d_ref[...] * u
    hg = jax.nn.gelu(y)
    out = hg * jax.nn.sigmoid(_dot(hg.astype(BF16), wg_ref[...]) + bg_ref[...])
    ms = jnp.mean(out * out, axis=-1, keepdims=True)
    out = (out * lax.rsqrt(ms + EPS) * gain_ref[...]).astype(o_ref.dtype)
    o_ref[...] = out.reshape(tc, nb * S5_CHANNELS)


def _s5_weights(a_re, a_im, log_step, b_re, b_im, c_re, c_im):
    lam_re = jnp.minimum(a_re, S5_MIN_NEG)
    lam_im = a_im
    step = jnp.exp(log_step)[:, None]
    mag = jnp.exp(lam_re * step)
    abar_re = mag * jnp.cos(lam_im * step)
    abar_im = mag * jnp.sin(lam_im * step)
    den = lam_re * lam_re + lam_im * lam_im
    nr = abar_re - 1.0
    coef_re = (nr * lam_re + abar_im * lam_im) / den
    coef_im = (abar_im * lam_re - nr * lam_im) / den
    bb_re = coef_re[..., None] * b_re - coef_im[..., None] * b_im
    bb_im = coef_re[..., None] * b_im + coef_im[..., None] * b_re
    eye = jnp.eye(S5_GROUPS, dtype=F32)
    bd = lambda m: jnp.einsum('gnp,gh->gphn', m, eye).reshape(S5_CHANNELS, S5_WIDTH)
    bmat = jnp.concatenate([bd(bb_re), bd(bb_im)], axis=1)
    cd = lambda m: jnp.einsum('gpn,gh->gnhp', m, eye).reshape(S5_WIDTH, S5_CHANNELS)
    cmat = jnp.concatenate([cd(c_re), -cd(c_im)], axis=0)
    return bmat.astype(BF16), abar_re.reshape(1, S5_WIDTH), abar_im.reshape(1, S5_WIDTH), cmat.astype(BF16)


def _s5_call(u_tb, nb, bmat, abar_re, abar_im, cmat, d_skip, w_glu, b_glu, gain, tc):
    S = u_tb.shape[0]
    C = S5_CHANNELS
    const = lambda shape: pl.BlockSpec(shape, lambda i: (0,) * len(shape))
    return pl.pallas_call(
        functools.partial(_s5_kernel, tc=tc, nb=nb),
        grid=(S // tc,),
        in_specs=[pl.BlockSpec((tc, nb * C), lambda i: (i, 0)),
                  const((C, 2 * S5_WIDTH)), const((nb, S5_WIDTH)), const((nb, S5_WIDTH)),
                  const((2 * S5_WIDTH, C)), const((1, C)), const((C, C)), const((1, C)), const((1, C))],
        out_specs=pl.BlockSpec((tc, nb * C), lambda i: (i, 0)),
        out_shape=jax.ShapeDtypeStruct((S, nb * C), BF16),
        scratch_shapes=[pltpu.VMEM((tc * nb, 2 * S5_WIDTH), F32), pltpu.VMEM((nb, 2 * S5_WIDTH), F32)],
        compiler_params=_params("arbitrary"),
        name="s5_mixer",
    )(u_tb, bmat, jnp.broadcast_to(abar_re, (nb, S5_WIDTH)), jnp.broadcast_to(abar_im, (nb, S5_WIDTH)),
      cmat, d_skip.reshape(1, C), w_glu.astype(BF16), b_glu.reshape(1, C), gain.reshape(1, C))


def _compress_kernel(x_ref, pe_ref, w1_ref, b1_ref, w2_ref, b2_ref, kg_ref, k_ref, v_ref):
    half = CMP_STRIDE * HEAD_DIM
    n = x_ref.shape[1]
    for seg in range(2 * KV_HEADS):
        kind = seg // KV_HEADS
        x = x_ref[seg]
        z0 = _dot((x + pe_ref[kind, 0:1]).astype(BF16), w1_ref[kind, 0:half])
        z1 = _dot((x + pe_ref[kind, 1:2]).astype(BF16), w1_ref[kind, half:2 * half])
        z = z0 + pltpu.roll(z1, n - 1, 0)
        act = jax.nn.gelu(z + b1_ref[kind])
        out = _dot(act.astype(BF16), w2_ref[kind]) + b2_ref[kind]
        if kind == 0:
            ms = jnp.mean(out * out, axis=-1, keepdims=True)
            k_ref[seg] = (out * lax.rsqrt(ms + EPS) * kg_ref[...]).astype(BF16)
        else:
            v_ref[seg - KV_HEADS] = _dot_nt(_eye(HEAD_DIM), out.astype(BF16)).astype(BF16)


def _compress_call(kvc, pe, w1, b1, w2, b2, k_gain0):
    B, _, S, _ = kvc.shape
    n = S // CMP_STRIDE
    half = CMP_STRIDE * HEAD_DIM
    x = kvc.reshape(B, 2 * KV_HEADS, n, half)
    const = lambda shape: pl.BlockSpec(shape, lambda b: (0,) * len(shape))
    out = jax.ShapeDtypeStruct((B, KV_HEADS, n, HEAD_DIM), BF16)
    return pl.pallas_call(
        _compress_kernel,
        grid=(B,),
        in_specs=[pl.BlockSpec((None, 2 * KV_HEADS, n, half), lambda b: (b, 0, 0, 0)),
                  const((2, 2, half)), const((2, 2 * half, CMP_HIDDEN)), const((2, 1, CMP_HIDDEN)),
                  const((2, CMP_HIDDEN, HEAD_DIM)), const((2, 1, HEAD_DIM)), const((1, HEAD_DIM))],
        out_specs=[pl.BlockSpec((None, KV_HEADS, n, HEAD_DIM), lambda b: (b, 0, 0, 0)),
                   pl.BlockSpec((None, KV_HEADS, HEAD_DIM, n), lambda b: (b, 0, 0, 0))],
        out_shape=[out, jax.ShapeDtypeStruct((B, KV_HEADS, HEAD_DIM, n), BF16)],
        compiler_params=_params("arbitrary"),
        name="nsa_compress",
    )(x, pe.reshape(2, 2, half), w1.astype(BF16), b1.reshape(2, 1, CMP_HIDDEN), w2.astype(BF16),
      b2.reshape(2, 1, HEAD_DIM), k_gain0.reshape(1, HEAD_DIM))


def _cmp_select_kernel(q_ref, k_ref, vt_ref, cov_ref, gate_ref, o_ref, qa_ref, *, tq, topk):
    n_cmp = k_ref.shape[1]
    n_sel = cov_ref.shape[0]
    q0 = pl.program_id(1) * tq
    cidx = lax.broadcasted_iota(jnp.int32, (n_cmp, GROUP * tq), 0)
    t_col = q0 + (lax.broadcasted_iota(jnp.int32, (n_cmp, GROUP * tq), 1) & (tq - 1))
    ok = cidx * CMP_STRIDE + (CMP_BLOCK - 1) <= t_col
    t_row = q0 + (lax.broadcasted_iota(jnp.int32, (1, GROUP * tq), 1) & (tq - 1))
    any_ok = jnp.where(t_row >= CMP_BLOCK - 1, 1.0, 0.0)

    j = lax.broadcasted_iota(jnp.int32, (n_sel, tq), 0)
    qblk = (q0 + lax.broadcasted_iota(jnp.int32, (n_sel, tq), 1)) >> (SEL_BLOCK.bit_length() - 1)
    forced = (j == 0) | (j == qblk) | (j == qblk - 1)
    eye = _eye(HEAD_DIM)

    outs = []
    for h in range(KV_HEADS):
        q = q_ref[h * GROUP:(h + 1) * GROUP].reshape(GROUP * tq, HEAD_DIM)
        qt = _dot_nt(eye, q).astype(BF16)
        s = jnp.where(ok, _dot(k_ref[h], qt), NEG_INF)
        m = jnp.max(s, axis=0, keepdims=True)
        e = jnp.exp2(s - m)
        p = e * (any_ok / jnp.sum(e, axis=0, keepdims=True))
        ot = _dot(vt_ref[h], p.astype(BF16))
        for g in range(GROUP):
            col = (h * GROUP + g) * N_BRANCH
            outs.append(ot[:, g * tq:(g + 1) * tq] * gate_ref[col:col + 1, :])

        psum = p[:, 0:tq] + p[:, tq:2 * tq] + p[:, 2 * tq:3 * tq]
        p_hi, p_lo = _split_bf16(psum)
        imp = _dot(cov_ref[...], p_hi) + _dot(cov_ref[...], p_lo)
        imp = jnp.where(forced, imp + FORCE_BONUS, imp)
        imp = jnp.where(j <= qblk, imp, NEG_INF)
        rank = jnp.zeros((n_sel, tq), F32)
        for i in range(n_sel):
            r = imp[i:i + 1, :]
            rank = rank + jnp.where(j > i, jnp.where(r >= imp, 1.0, 0.0), jnp.where(r > imp, 1.0, 0.0))
        bias = jnp.where(rank < topk, 0.0, UNSELECTED_BIAS).astype(BF16)
        if n_sel < HEAD_DIM:
            bias = jnp.concatenate([bias, jnp.zeros((HEAD_DIM - n_sel, tq), BF16)], axis=0)
        qa_ref[h] = jnp.concatenate([qt, jnp.concatenate([bias] * GROUP, axis=1)], axis=0)
    o_ref[...] = jnp.concatenate(outs, axis=0).T


def _cover_matrix_t(n_cmp, n_sel):
    cs = np.arange(n_cmp)[None, :] * CMP_STRIDE
    ss = np.arange(n_sel)[:, None] * SEL_BLOCK
    cover = np.clip(np.minimum(cs + CMP_BLOCK, ss + SEL_BLOCK) - np.maximum(cs, ss), 0, None)
    return jnp.asarray(cover / CMP_BLOCK, BF16)


def _cmp_select_call(q, k_cmp, v_cmp_t, gates):
    B, _, S, _ = q.shape
    tq = SEL_TILE
    n_cmp = k_cmp.shape[2]
    n_sel = S // SEL_BLOCK
    assert n_sel <= HEAD_DIM
    topk = min(SEL_TOPK, n_sel)
    return pl.pallas_call(
        functools.partial(_cmp_select_kernel, tq=tq, topk=topk),
        grid=(B, S // tq),
        in_specs=[pl.BlockSpec((None, Q_HEADS, tq, HEAD_DIM), lambda b, i: (b, 0, i, 0)),
                  pl.BlockSpec((None, KV_HEADS, n_cmp, HEAD_DIM), lambda b, i: (b, 0, 0, 0)),
                  pl.BlockSpec((None, KV_HEADS, HEAD_DIM, n_cmp), lambda b, i: (b, 0, 0, 0)),
                  pl.BlockSpec((n_sel, n_cmp), lambda b, i: (0, 0)),
                  pl.BlockSpec((None, LANES, tq), lambda b, i: (b, 0, i))],
        out_specs=[pl.BlockSpec((None, tq, ATT_WIDTH), lambda b, i: (b, i, 0)),
                   pl.BlockSpec((None, KV_HEADS, None, 2 * HEAD_DIM, GROUP * tq), lambda b, i: (b, 0, i, 0, 0))],
        out_shape=[jax.ShapeDtypeStruct((B, S, ATT_WIDTH), F32),
                   jax.ShapeDtypeStruct((B, KV_HEADS, S // tq, 2 * HEAD_DIM, GROUP * tq), BF16)],
        compiler_params=_params("arbitrary", "arbitrary"),
        name="nsa_cmp_select",
    )(q, k_cmp, v_cmp_t, _cover_matrix_t(n_cmp, n_sel), gates)


def _selected_kernel(qa_ref, k_ref, vt_ref, gate_ref, o_ref, m_ref, l_ref, acc_ref, *, tq):
    i = pl.program_id(1)
    width = GROUP * tq
    krow = lax.broadcasted_iota(jnp.int32, (tq, width), 0)
    qcol = lax.broadcasted_iota(jnp.int32, (tq, width), 1) & (tq - 1)
    causal = krow <= qcol

    m_ref[...] = jnp.full_like(m_ref, NEG_INF)
    l_ref[...] = jnp.zeros_like(l_ref)
    acc_ref[...] = jnp.zeros_like(acc_ref)

    def update(k0, nk, diagonal):
        k0 = pl.multiple_of(k0, tq)
        s_all = [_dot(k_ref[h, pl.ds(k0, nk), :], qa_ref[h]) for h in range(KV_HEADS)]
        p_all = []
        for h in range(KV_HEADS):
            s = jnp.where(causal, s_all[h], NEG_INF) if diagonal else s_all[h]
            m_prev = m_ref[h]
            m_new = jnp.maximum(m_prev, jnp.max(s, axis=0, keepdims=True))
            alpha = jnp.exp2(m_prev - m_new)
            p = jnp.exp2(s - m_new)
            l_ref[h] = alpha * l_ref[h] + jnp.sum(p, axis=0, keepdims=True)
            m_ref[h] = m_new
            p_all.append((alpha, p.astype(BF16)))
        for h in range(KV_HEADS):
            alpha, p = p_all[h]
            acc_ref[h] = alpha * acc_ref[h] + _dot(vt_ref[h, :, pl.ds(k0, nk)], p)

    def body(j, carry):
        update(j * (2 * tq), 2 * tq, False)
        return carry

    lax.fori_loop(0, i >> 1, body, 0)

    @pl.when((i & 1) == 1)
    def _():
        update((i - 1) * tq, tq, False)

    update(i * tq, tq, True)

    outs = []
    for h in range(KV_HEADS):
        ot = acc_ref[h] / l_ref[h]
        for g in range(GROUP):
            col = (h * GROUP + g) * N_BRANCH + 1
            outs.append(ot[:, g * tq:(g + 1) * tq] * gate_ref[col:col + 1, :])
    o_ref[...] = jnp.concatenate(outs, axis=0).T


def _selected_call(q_aug, k_aug, v_t, gates):
    B, _, n_tiles, _, width = q_aug.shape
    tq = width // GROUP
    S = n_tiles * tq
    return pl.pallas_call(
        functools.partial(_selected_kernel, tq=tq),
        grid=(B, n_tiles),
        in_specs=[pl.BlockSpec((None, KV_HEADS, None, 2 * HEAD_DIM, width), lambda b, i: (b, 0, i, 0, 0)),
                  pl.BlockSpec((None, KV_HEADS, S, 2 * HEAD_DIM), lambda b, i: (b, 0, 0, 0)),
                  pl.BlockSpec((None, KV_HEADS, HEAD_DIM, S), lambda b, i: (b, 0, 0, 0)),
                  pl.BlockSpec((None, LANES, tq), lambda b, i: (b, 0, i))],
        out_specs=pl.BlockSpec((None, tq, ATT_WIDTH), lambda b, i: (b, i, 0)),
        out_shape=jax.ShapeDtypeStruct((B, S, ATT_WIDTH), F32),
        scratch_shapes=[pltpu.VMEM((KV_HEADS, 1, width), F32), pltpu.VMEM((KV_HEADS, 1, width), F32),
                        pltpu.VMEM((KV_HEADS, HEAD_DIM, width), F32)],
        compiler_params=_params("arbitrary", "arbitrary"),
        name="nsa_selected",
    )(q_aug, k_aug, v_t, gates)


def _mix_ffn_kernel(oa_ref, ob_ref, c1_ref, c2_ref, c3_ref, gc_ref, wo_ref, x_ref, ga1_ref,
                    g2_ref, sc_ref, sh_ref, ga2_ref, w1_ref, w2_ref, o_ref, *, tf):
    oc = c1_ref[...] + c2_ref[...] + c3_ref[...]
    ms = jnp.mean(oc * oc, axis=-1, keepdims=True)
    oc = (oc * lax.rsqrt(ms + EPS) * gc_ref[...]).astype(BF16)
    a_end = ATT_WIDTH
    b_end = ATT_WIDTH + S5_CHANNELS
    y = (_dot(oa_ref[...], wo_ref[0:a_end]) + _dot(ob_ref[...], wo_ref[a_end:b_end])
         + _dot(oc, wo_ref[b_end:]))
    x = x_ref[...] + ga1_ref[...] * y

    ms = jnp.mean(x * x, axis=-1, keepdims=True)
    h = x * lax.rsqrt(ms + EPS) * g2_ref[...]
    h = (h * (1.0 + sc_ref[...]) + sh_ref[...]).astype(BF16)
    acc = None
    for f0 in range(0, w1_ref.shape[1], tf):
        a = jnp.maximum(_dot(h, w1_ref[:, f0:f0 + tf]), 0.0)
        y = _dot((a * a).astype(BF16), w2_ref[f0:f0 + tf, :])
        acc = y if acc is None else acc + y
    o_ref[...] = x + ga2_ref[...] * acc


def _mix_ffn_call(o_a, o_b_tb, o_cmp, o_slc, o_win, gain_c, w_out, x, ga1, gain2, sc2, sh2, ga2, w1, w2, tm, tf):
    B, S, D = x.shape
    F = w1.shape[1]
    att = pl.BlockSpec((None, tm, ATT_WIDTH), lambda b, i: (b, i, 0))
    vec = pl.BlockSpec((None, 1, D), lambda b, i: (b, 0, 0))
    resident = dict(pipeline_mode=pl.Buffered(1))
    return pl.pallas_call(
        functools.partial(_mix_ffn_kernel, tf=tf),
        grid=(B, S // tm),
        in_specs=[att, pl.BlockSpec((tm, S5_CHANNELS), lambda b, i: (i, b)), att, att, att,
                  pl.BlockSpec((1, ATT_WIDTH), lambda b, i: (0, 0)),
                  pl.BlockSpec(w_out.shape, lambda b, i: (0, 0), **resident),
                  pl.BlockSpec((None, tm, D), lambda b, i: (b, i, 0)), vec,
                  pl.BlockSpec((1, D), lambda b, i: (0, 0)), vec, vec, vec,
                  pl.BlockSpec((D, F), lambda b, i: (0, 0), **resident),
                  pl.BlockSpec((F, D), lambda b, i: (0, 0), **resident)],
        out_specs=pl.BlockSpec((None, tm, D), lambda b, i: (b, i, 0)),
        out_shape=jax.ShapeDtypeStruct((B, S, D), F32),
        compiler_params=_params("arbitrary", "arbitrary"),
        name="mix_ffn",
    )(o_a, o_b_tb, o_cmp, o_slc, o_win, gain_c.reshape(1, ATT_WIDTH), w_out, x, ga1,
      gain2.reshape(1, D), sc2, sh2, ga2, w1, w2)


def _layer(x, mod, p, tiles):
    B, S, D = x.shape
    sh1, sc1, ga1, sh2, sc2, ga2 = [m.reshape(B, 1, D) for m in jnp.split(mod, 6, axis=-1)]

    w_in, qk_gain = _inproj_weights(p["w_in"], p["a_q_gain"], p["a_k_gain"], p["c_q_gain"], p["c_k_gain"])
    (qa, qc, ka, ksel, kwin, va, vsel, vwin, kvc, su, gates) = _inproj_call(
        x, p["norm1_g"], sc1, sh1, w_in, qk_gain, tiles["tm_in"])

    a_end = ATT_WIDTH
    b_end = ATT_WIDTH + S5_CHANNELS
    o_a = _band_call(qa, ka, va, window=A_WINDOW, sinks=p["a_sinks"], norm_gain=p["out_norm_g"][:a_end],
                     out_dtype=BF16)

    bmat, abar_re, abar_im, cmat = _s5_weights(p["s5_a_re"], p["s5_a_im"], p["s5_log_step"],
                                               p["s5_b_re"], p["s5_b_im"], p["s5_c_re"], p["s5_c_im"])
    o_b = _s5_call(su, B, bmat, abar_re, abar_im, cmat, p["s5_d"],
                   p["s5_w_glu"], p["s5_b_glu"], p["out_norm_g"][a_end:b_end], tiles["tc_s5"])

    k_cmp, v_cmp = _compress_call(kvc, p["cmp_pe"], p["cmp_w1"], p["cmp_b1"], p["cmp_w2"], p["cmp_b2"],
                                  p["c_k_gain"][0])
    o_cmp, q_aug = _cmp_select_call(qc, k_cmp, v_cmp, gates)
    o_slc = _selected_call(q_aug, ksel, vsel, gates)
    o_win = _band_call(q_aug, kwin, vwin, window=C_WINDOW, gates=gates, gate_branch=2)

    return _mix_ffn_call(o_a, o_b, o_cmp, o_slc, o_win, p["out_norm_g"][b_end:], p["w_out"].astype(BF16), x, ga1,
                         p["norm2_g"], sc2, sh2, ga2, p["w_ff1"].astype(BF16), p["w_ff2"].astype(BF16),
                         tiles["tm_ffn"], tiles["tf_ffn"])


def _tiles(S):
    return dict(tm_in=min(512, S), tc_s5=min(128, S), tm_ffn=min(512, S), tf_ffn=1024)


def kernel(x, c, norm1_g, norm2_g, w_ada, b_ada, w_in, a_q_gain, a_k_gain, a_sinks, s5_a_re, s5_a_im, s5_log_step, s5_b_re, s5_b_im, s5_c_re, s5_c_im, s5_d, s5_w_glu, s5_b_glu, c_q_gain, c_k_gain, cmp_pe, cmp_w1, cmp_b1, cmp_w2, cmp_b2, out_norm_g, w_out, w_ff1, w_ff2):
    per_layer = dict(norm1_g=norm1_g, norm2_g=norm2_g, w_in=w_in, a_q_gain=a_q_gain, a_k_gain=a_k_gain,
                     a_sinks=a_sinks, s5_a_re=s5_a_re, s5_a_im=s5_a_im, s5_log_step=s5_log_step,
                     s5_b_re=s5_b_re, s5_b_im=s5_b_im, s5_c_re=s5_c_re, s5_c_im=s5_c_im, s5_d=s5_d,
                     s5_w_glu=s5_w_glu, s5_b_glu=s5_b_glu, c_q_gain=c_q_gain, c_k_gain=c_k_gain,
                     cmp_pe=cmp_pe, cmp_w1=cmp_w1, cmp_b1=cmp_b1, cmp_w2=cmp_w2, cmp_b2=cmp_b2,
                     out_norm_g=out_norm_g, w_out=w_out, w_ff1=w_ff1, w_ff2=w_ff2)
    depth = w_in.shape[0]
    mod = _ada_call(c, w_ada, b_ada)
    tiles = _tiles(x.shape[1])
    for l in range(depth):
        x = _layer(x, mod[l], {k: v[l] for k, v in per_layer.items()}, tiles)
    return x
```
